```python
import jax, jax.numpy as jnp
from jax import lax
import numpy as np

D_MODEL = 2048
BATCH = 2
SEQ = 4096
DEPTH = 2

RWKV_HEADS = 16
RWKV_HEAD_DIM = 64
D_RWKV = RWKV_HEADS * RWKV_HEAD_DIM
DECAY_LORA = 64
ICLR_LORA = 64
GATE_LORA = 160
LNX_EPS = 64e-5
POOL_WINDOWS = (2, 4, 8, 16)
N_POOL_GROUPS = 4
POOL_GROUP_DIM = 256
D_POOL = N_POOL_GROUPS * POOL_GROUP_DIM
N_EXPERTS = 16
EXPERT_FF = 2816
EC_CAPACITY_FACTOR = 2
NORM_EPS = 1e-6
D_SHIFT = 3 * D_RWKV + 2 * DECAY_LORA + 2 * ICLR_LORA + GATE_LORA
D_IN = D_SHIFT + D_POOL + 2 * D_MODEL

kernel_name = "hybrid_rwkv7_pool_ec_encoder"


def rms_norm(x, g):
    xf = x.astype(jnp.float32)
    y = xf * lax.rsqrt(jnp.mean(xf * xf, axis=-1, keepdims=True) + NORM_EPS)
    return (y * g.astype(jnp.float32)).astype(x.dtype)


def centred_shift(p):
    prev = jnp.pad(p[:, :-1], ((0, 0), (1, 0), (0, 0)))
    nxt = jnp.pad(p[:, 1:], ((0, 0), (0, 1), (0, 0)))
    return 0.5 * (prev + nxt) - p


def _heads(t):
    return t.reshape(t.shape[:-1] + (RWKV_HEADS, RWKV_HEAD_DIM))


def _dir_stack(f, b):
    return jnp.stack([f, jnp.flip(b, axis=1)], axis=0)


def rwkv7_scan(r, w, kk, b, k, v):
    xs = tuple(jnp.moveaxis(t, 2, 0) for t in (r, w, kk, b, k, v))
    n_dir, bsz, _, h, n = r.shape

    def step(S, inp):
        r_t, w_t, kk_t, b_t, k_t, v_t = inp
        sa = jnp.einsum("dbhvk,dbhk->dbhv", S, kk_t)
        S = (S * w_t[..., None, :]
             - sa[..., :, None] * b_t[..., None, :]
             + v_t[..., :, None] * k_t[..., None, :])
        y = jnp.einsum("dbhvk,dbhk->dbhv", S, r_t)
        return S, y

    S0 = jnp.zeros((n_dir, bsz, h, n, n), jnp.float32)
    _, y = lax.scan(step, S0, xs)
    return jnp.moveaxis(y, 0, 2)


def rwkv7_branch(ps, w0, w_lora_up, a0, a_lora_up, g_lora_up, k_k, k_a, r_k, lnx_g, lnx_b):
    f32 = jnp.float32
    ps = ps.astype(f32)
    cuts = [int(c) for c in np.cumsum([D_RWKV, D_RWKV, D_RWKV, DECAY_LORA, DECAY_LORA, ICLR_LORA, ICLR_LORA])]
    r, k, v, wd_f, wd_b, ad_f, ad_b, gd = jnp.split(ps, cuts, axis=-1)
    wd = jnp.stack([wd_f, wd_b], axis=0)
    ad = jnp.stack([ad_f, ad_b], axis=0)
    w_pre = w0.astype(f32)[:, None, None, :] + jnp.einsum("dbtl,dlc->dbtc", jnp.tanh(wd), w_lora_up.astype(f32))
    decay = jnp.exp(-jnp.exp(-jax.nn.softplus(-w_pre) - 0.5))
    a = jax.nn.sigmoid(a0.astype(f32)[:, None, None, :] + jnp.einsum("dbtl,dlc->dbtc", ad, a_lora_up.astype(f32)))
    g = jnp.einsum("btl,lc->btc", jax.nn.sigmoid(gd), g_lora_up.astype(f32))
    kk = _heads(k * k_k.astype(f32))
    kk = kk / jnp.maximum(jnp.sqrt(jnp.sum(kk * kk, axis=-1, keepdims=True)), 1e-12)
    k_dir = k[None] * (1.0 + (a - 1.0) * k_a.astype(f32))
    a_h, k_h, dec_h = _heads(a), _heads(k_dir), _heads(decay)
    r_h, v_h = _heads(r), _heads(v)
    y2 = rwkv7_scan(
        _dir_stack(r_h, r_h),
        _dir_stack(dec_h[0], dec_h[1]),
        _dir_stack(kk, kk),
        _dir_stack(kk * a_h[0], kk * a_h[1]),
        _dir_stack(k_h[0], k_h[1]),
        _dir_stack(v_h, v_h),
    )
    y = y2[0] + jnp.flip(y2[1], axis=1)
    mu = jnp.mean(y, axis=-1, keepdims=True)
    var = jnp.mean(jnp.square(y - mu), axis=-1, keepdims=True)
    y = (y - mu) * lax.rsqrt(var + LNX_EPS)
    y = y.reshape(y.shape[:2] + (D_RWKV,)) * lnx_g.astype(f32) + lnx_b.astype(f32)
    k_bonus = 0.5 * (k_h[0] + k_h[1])
    bonus = jnp.sum(r_h * k_bonus * r_k.astype(f32), axis=-1, keepdims=True) * v_h
    y = (y + bonus.reshape(y.shape)) * g
    return y


def centred_pool_minus_self(p, window):
    bsz, t_len, c = p.shape
    pf = p.astype(jnp.float32)
    cs = jnp.concatenate([jnp.zeros((bsz, 1, c), jnp.float32), jnp.cumsum(pf, axis=1)], axis=1)
    t = jnp.arange(t_len)
    half = window // 2
    lo = jnp.clip(t - half, 0, t_len)
    hi = jnp.clip(t + half, 0, t_len)
    cnt = (hi - lo).astype(jnp.float32)
    mean = (jnp.take(cs, hi, axis=1) - jnp.take(cs, lo, axis=1)) / cnt[None, :, None]
    return mean - pf


def pool_branch(p, pool_w, pool_scale):
    groups = jnp.split(p, N_POOL_GROUPS, axis=-1)
    z = jnp.stack([centred_pool_minus_self(gp, w) for gp, w in zip(groups, POOL_WINDOWS)], axis=2)
    y = jnp.einsum("btgc,gcd->btgd", z, pool_w.astype(jnp.float32))
    y = y.reshape(y.shape[:2] + (D_POOL,)) * pool_scale.astype(jnp.float32)
    return y.astype(p.dtype)


def ec_moe(u, w_router, w_gate, w_up, w_down):
    bsz, t_len, _ = u.shape
    cap = EC_CAPACITY_FACTOR * t_len // N_EXPERTS
    aff = jax.nn.softmax(jnp.einsum("btd,de->bte", u, w_router).astype(jnp.float32), axis=-1)
    gates, idx = lax.top_k(jnp.swapaxes(aff, 1, 2), cap)
    bidx = jnp.arange(bsz)[:, None, None]
    xs = u[bidx, idx]
    h = jax.nn.silu(jnp.einsum("becd,edf->becf", xs, w_gate)) * jnp.einsum("becd,edf->becf", xs, w_up)
    y = jnp.einsum("becf,efd->becd", h, w_down) * gates[..., None].astype(u.dtype)
    return jnp.zeros_like(u).at[bidx, idx].add(y)


def setup_inputs(seed: int = 0) -> dict:
    key = jax.random.key(seed)
    ks = jax.random.split(key, 32)
    f32 = jnp.float32
    L, D = DEPTH, D_MODEL

    def nrm(k, shape, scale):
        return jax.random.normal(k, shape, f32) * scale

    w0_base = jnp.tile(jnp.linspace(-6.0, 1.0, RWKV_HEAD_DIM, dtype=f32), RWKV_HEADS)
    return {
        "x": nrm(ks[0], (BATCH, SEQ, D), 1.0),
        "norm_mix_g": 1.0 + nrm(ks[1], (L, D), 0.02),
        "w_in": nrm(ks[2], (L, D, D_IN), D ** -0.5),
        "mu_shift": jax.random.uniform(ks[3], (L, D_SHIFT), f32, 0.2, 0.8),
        "w0": w0_base + nrm(ks[4], (L, 2, D_RWKV), 0.1),
        "w_lora_up": nrm(ks[5], (L, 2, DECAY_LORA, D_RWKV), 0.1 * DECAY_LORA ** -0.5),
        "a0": nrm(ks[6], (L, 2, D_RWKV), 0.1),
        "a_lora_up": nrm(ks[7], (L, 2, ICLR_LORA, D_RWKV), 0.1 * ICLR_LORA ** -0.5),
        "g_lora_up": nrm(ks[8], (L, GATE_LORA, D_RWKV), GATE_LORA ** -0.5),
        "k_k": 0.85 + nrm(ks[9], (L, D_RWKV), 0.05),
        "k_a": 1.0 + nrm(ks[10], (L, D_RWKV), 0.05),
        "r_k": nrm(ks[11], (L, RWKV_HEADS, RWKV_HEAD_DIM), 0.1),
        "lnx_g": 1.0 + nrm(ks[12], (L, D_RWKV), 0.02),
        "lnx_b": nrm(ks[13], (L, D_RWKV), 0.02),
        "pool_w": nrm(ks[14], (L, N_POOL_GROUPS, POOL_GROUP_DIM, POOL_GROUP_DIM), POOL_GROUP_DIM ** -0.5),
        "pool_scale": 1.0 + nrm(ks[15], (L, D_POOL), 0.1),
        "w_up_a": nrm(ks[16], (L, D_RWKV, D), D_RWKV ** -0.5),
        "w_up_b": nrm(ks[17], (L, D_POOL, D), D_POOL ** -0.5),
        "w_o": nrm(ks[18], (L, D, D), D ** -0.5),
        "norm_moe_g": 1.0 + nrm(ks[19], (L, D), 0.02),
        "w_router": nrm(ks[20], (L, D, N_EXPERTS), D ** -0.5),
        "w_gate_e": nrm(ks[21], (L, N_EXPERTS, D, EXPERT_FF), D ** -0.5),
        "w_up_e": nrm(ks[22], (L, N_EXPERTS, D, EXPERT_FF), D ** -0.5),
        "w_down_e": nrm(ks[23], (L, N_EXPERTS, EXPERT_FF, D), EXPERT_FF ** -0.5),
        "final_g": 1.0 + nrm(ks[24], (D,), 0.02),
    }


def reference(x, norm_mix_g, w_in, mu_shift, w0, w_lora_up, a0, a_lora_up, g_lora_up, k_k, k_a, r_k,
              lnx_g, lnx_b, pool_w, pool_scale, w_up_a, w_up_b, w_o, norm_moe_g, w_router,
              w_gate_e, w_up_e, w_down_e, final_g):
    for l in range(DEPTH):
        u = rms_norm(x, norm_mix_g[l])
        proj = jnp.einsum("btd,dn->btn", u, w_in[l])
        p_rwkv = proj[..., :D_SHIFT]
        p_rwkv = p_rwkv + mu_shift[l].astype(p_rwkv.dtype) * centred_shift(p_rwkv)
        p_pool = proj[..., D_SHIFT:D_SHIFT + D_POOL]
        gate_a = proj[..., D_SHIFT + D_POOL:D_SHIFT + D_POOL + D_MODEL]
        gate_b = proj[..., D_SHIFT + D_POOL + D_MODEL:]
        y_a = rwkv7_branch(p_rwkv, w0[l], w_lora_up[l], a0[l], a_lora_up[l], g_lora_up[l],
                           k_k[l], k_a[l], r_k[l], lnx_g[l], lnx_b[l]).astype(x.dtype)
        y_b = pool_branch(p_pool, pool_w[l], pool_scale[l])
        merged = (jax.nn.sigmoid(gate_a) * jnp.einsum("btc,cd->btd", y_a, w_up_a[l])
                  + jax.nn.sigmoid(gate_b) * jnp.einsum("btc,cd->btd", y_b, w_up_b[l]))
        x = x + jnp.einsum("btd,de->bte", merged, w_o[l])
        v = rms_norm(x, norm_moe_g[l])
        x = x + ec_moe(v, w_router[l], w_gate_e[l], w_up_e[l], w_down_e[l])
    return rms_norm(x, final_g)
```

```python
import functools

import jax
import jax.numpy as jnp
from jax import lax
from jax.experimental import pallas as pl
from jax.experimental.pallas import tpu as pltpu

F32 = jnp.float32
BF16 = jnp.bfloat16

HEAD_DIM = 64
N_HEADS = 16
D_RWKV = N_HEADS * HEAD_DIM
LORA_PAD = 128
GATE_PAD = 256
POOL_WINDOWS = (2, 4, 8, 16)
POOL_GROUP = 256
N_EXPERTS = 16
CAPACITY_FACTOR = 2
NORM_EPS = 1e-6
LNX_EPS = 64e-5

CHUNK = 64
GROUP_LANES = 256
HEADS_PER_GROUP = GROUP_LANES // HEAD_DIM
HALO = 8
VMEM_LIMIT = 56 * 1024 * 1024


def _params(sem):
    return pltpu.CompilerParams(dimension_semantics=sem, vmem_limit_bytes=VMEM_LIMIT)


def _dot(a, b):
    return jnp.dot(a.astype(BF16), b.astype(BF16), preferred_element_type=F32)


def _dot_nt(a, b):
    return lax.dot_general(a.astype(BF16), b.astype(BF16), (((1,), (1,)), ((), ())),
                           preferred_element_type=F32)


def _dot_tn(a, b):
    return lax.dot_general(a.astype(BF16), b.astype(BF16), (((0,), (0,)), ((), ())),
                           preferred_element_type=F32)


def _split2(x):
    hi = x.astype(BF16)
    lo = (x - hi.astype(F32)).astype(BF16)
    return hi, lo


def _split3(x):
    hi = x.astype(BF16)
    r1 = x - hi.astype(F32)
    mid = r1.astype(BF16)
    lo = (r1 - mid.astype(F32)).astype(BF16)
    return hi, mid, lo


def _head_ones(n):
    r = lax.broadcasted_iota(jnp.int32, (n, n), 0) // HEAD_DIM
    c = lax.broadcasted_iota(jnp.int32, (n, n), 1) // HEAD_DIM
    return r == c


def _head_sum(x, ones_bf16):
    hi, lo = _split2(x)
    return (jnp.dot(hi, ones_bf16, preferred_element_type=F32)
            + jnp.dot(lo, ones_bf16, preferred_element_type=F32))


def _rmsnorm_kernel(x_ref, g_ref, o_ref):
    x = x_ref[...]
    ms = jnp.mean(x * x, axis=-1, keepdims=True)
    o_ref[...] = (x * lax.rsqrt(ms + NORM_EPS) * g_ref[...]).astype(o_ref.dtype)


def _rmsnorm(x, g, out_dtype, tm=512):
    n, d = x.shape
    return pl.pallas_call(
        _rmsnorm_kernel,
        grid=(n // tm,),
        in_specs=[pl.BlockSpec((tm, d), lambda i: (i, 0)), pl.BlockSpec((1, d), lambda i: (0, 0))],
        out_specs=pl.BlockSpec((tm, d), lambda i: (i, 0)),
        out_shape=jax.ShapeDtypeStruct((n, d), out_dtype),
        compiler_params=_params(("parallel",)),
        name="rmsnorm",
    )(x, g.reshape(1, d))


def _mm_kernel(a_ref, w_ref, o_ref):
    o_ref[...] = jnp.dot(a_ref[...], w_ref[...], preferred_element_type=F32).astype(o_ref.dtype)


def _mm_res_kernel(a_ref, w_ref, r_ref, o_ref):
    o_ref[...] = r_ref[...] + jnp.dot(a_ref[...], w_ref[...], preferred_element_type=F32)


def _matmul(a, w, out_dtype=F32, residual=None, tm=512, tn=512, name="matmul"):
    m, k = a.shape
    nn = w.shape[1]
    tn = min(tn, nn)
    in_specs = [pl.BlockSpec((tm, k), lambda j, i: (i, 0)), pl.BlockSpec((k, tn), lambda j, i: (0, j))]
    args = [a, w]
    kern = _mm_kernel
    if residual is not None:
        in_specs.append(pl.BlockSpec((tm, tn), lambda j, i: (i, j)))
        args.append(residual)
        kern = _mm_res_kernel
    return pl.pallas_call(
        kern,
        grid=(nn // tn, m // tm),
        in_specs=in_specs,
        out_specs=pl.BlockSpec((tm, tn), lambda j, i: (i, j)),
        out_shape=jax.ShapeDtypeStruct((m, nn), out_dtype),
        compiler_params=_params(("parallel", "parallel")),
        name=name,
    )(*args)


def _token_shift(p_ref, prev_ref, next_ref, mu, first, last):
    p = p_ref[...]
    tm = p.shape[0]
    prev_row = jnp.where(first, 0.0, prev_ref[HALO - 1:HALO, :])
    next_row = jnp.where(last, 0.0, next_ref[0:1, :])
    rows = lax.broadcasted_iota(jnp.int32, p.shape, 0)
    p_prev = jnp.where(rows == 0, prev_row, pltpu.roll(p, 1, 0))
    p_next = jnp.where(rows == tm - 1, next_row, pltpu.roll(p, tm - 1, 0))
    return p + mu * (0.5 * (p_prev + p_next) - p)


def _softplus(x):
    return jnp.maximum(x, 0.0) + jnp.log(1.0 + jnp.exp(-jnp.abs(x)))


def _sigmoid(x):
    return 1.0 / (1.0 + jnp.exp(-x))


def _prep_kernel(tiles_per_seq,
                 r_ref, rp_ref, rn_ref, k_ref, kp_ref, kn_ref, v_ref, vp_ref, vn_ref,
                 lo_ref, lop_ref, lon_ref,
                 mur_ref, muk_ref, muv_ref, mulo_ref,
                 w0_ref, wup_ref, a0_ref, aup_ref, gup_ref, kk_ref, ka_ref, rk_ref,
                 o_r, o_kk, o_v, o_lwf, o_lwb, o_kf, o_kb, o_bf, o_bb, o_g, o_bonus):
    i = pl.program_id(0)
    first = (i % tiles_per_seq) == 0
    last = (i % tiles_per_seq) == tiles_per_seq - 1
    r = _token_shift(r_ref, rp_ref, rn_ref, mur_ref[...], first, last)
    k = _token_shift(k_ref, kp_ref, kn_ref, muk_ref[...], first, last)
    v = _token_shift(v_ref, vp_ref, vn_ref, muv_ref[...], first, last)
    lo = _token_shift(lo_ref, lop_ref, lon_ref, mulo_ref[...], first, last)
    ones = _head_ones(GROUP_LANES).astype(BF16)

    kk0 = k * kk_ref[...]
    nrm = jnp.sqrt(_head_sum(kk0 * kk0, ones))
    kk = kk0 / jnp.maximum(nrm, 1e-12)
    o_r[...] = r
    o_kk[...] = kk
    o_v[...] = v

    k_dirs = []
    for d, (o_lw, o_k, o_b) in enumerate(((o_lwf, o_kf, o_bf), (o_lwb, o_kb, o_bb))):
        wd = lo[:, d * LORA_PAD:(d + 1) * LORA_PAD]
        ad = lo[:, (2 + d) * LORA_PAD:(3 + d) * LORA_PAD]
        w_pre = w0_ref[d:d + 1, :] + _dot(jnp.tanh(wd), wup_ref[d])
        o_lw[...] = -jnp.exp(-_softplus(-w_pre) - 0.5)
        a = _sigmoid(a0_ref[d:d + 1, :] + _dot(ad, aup_ref[d]))
        k_d = k * (1.0 + (a - 1.0) * ka_ref[...])
        o_k[...] = k_d
        o_b[...] = kk * a
        k_dirs.append(k_d)
    gd = lo[:, 4 * LORA_PAD:4 * LORA_PAD + GATE_PAD]
    o_g[...] = _dot(_sigmoid(gd), gup_ref[...])
    k_bonus = 0.5 * (k_dirs[0] + k_dirs[1])
    o_bonus[...] = _head_sum(r * k_bonus * rk_ref[...], ones) * v


def _rwkv_prep(p_rkv, p_lora, mu_rkv, mu_lora, w0, wup, a0, aup, gup, k_k, k_a, r_k, seq_len, tm=256):
    n = p_rkv.shape[0]
    ng = D_RWKV // GROUP_LANES
    nlo = p_lora.shape[1]
    tph = tm // HALO
    nh = n // HALO

    def main(col0):
        return pl.BlockSpec((tm, GROUP_LANES), lambda i, g: (i, col0 + g))

    def prev(col0):
        return pl.BlockSpec((HALO, GROUP_LANES), lambda i, g: (jnp.maximum(i * tph - 1, 0), col0 + g))

    def nxt(col0):
        return pl.BlockSpec((HALO, GROUP_LANES), lambda i, g: (jnp.minimum((i + 1) * tph, nh - 1), col0 + g))

    in_specs = []
    args = []
    for c in range(3):
        in_specs += [main(c * ng), prev(c * ng), nxt(c * ng)]
        args += [p_rkv, p_rkv, p_rkv]
    in_specs += [pl.BlockSpec((tm, nlo), lambda i, g: (i, 0)),
                 pl.BlockSpec((HALO, nlo), lambda i, g: (jnp.maximum(i * tph - 1, 0), 0)),
                 pl.BlockSpec((HALO, nlo), lambda i, g: (jnp.minimum((i + 1) * tph, nh - 1), 0))]
    args += [p_lora, p_lora, p_lora]
    for c in range(3):
        in_specs.append(pl.BlockSpec((1, GROUP_LANES), lambda i, g, c=c: (0, c * ng + g)))
        args.append(mu_rkv)
    in_specs.append(pl.BlockSpec((1, nlo), lambda i, g: (0, 0)))
    args.append(mu_lora)
    vec2 = pl.BlockSpec((2, GROUP_LANES), lambda i, g: (0, g))
    lora2 = pl.BlockSpec((2, LORA_PAD, GROUP_LANES), lambda i, g: (0, 0, g))
    vec1 = pl.BlockSpec((1, GROUP_LANES), lambda i, g: (0, g))
    in_specs += [vec2, lora2, vec2, lora2, pl.BlockSpec((GATE_PAD, GROUP_LANES), lambda i, g: (0, g)),
                 vec1, vec1, vec1]
    args += [w0, wup, a0, aup, gup, k_k, k_a, r_k]
    out_spec = pl.BlockSpec((tm, GROUP_LANES), lambda i, g: (i, g))
    out_shape = jax.ShapeDtypeStruct((n, D_RWKV), F32)
    return pl.pallas_call(
        functools.partial(_prep_kernel, seq_len // tm),
        grid=(n // tm, ng),
        in_specs=in_specs,
        out_specs=[out_spec] * 11,
        out_shape=[out_shape] * 11,
        compiler_params=_params(("parallel", "parallel")),
        name="rwkv_prep",
    )(*args)


def _stack_heads(x, blk):
    return jnp.where(blk, jnp.concatenate([x] * HEADS_PER_GROUP, axis=0), 0.0)


def _wkv_chunk(r, kk, v, logw, k, b, s_prev, reverse):
    c = CHUNK
    row = lax.broadcasted_iota(jnp.int32, (c, GROUP_LANES), 0)
    col = lax.broadcasted_iota(jnp.int32, (c, GROUP_LANES), 1) % c
    before = (col > row) if reverse else (col < row)
    same = col == row
    incl = before | same
    blk = _head_ones(GROUP_LANES)
    eye = jnp.where(same, 1.0, 0.0)

    tri = jnp.where(incl[:, :c], 1.0, 0.0).astype(BF16)
    g = sum(jnp.dot(tri, part, preferred_element_type=F32) for part in _split3(logw))
    g_tot = g[0:1, :] if reverse else g[c - 1:c, :]
    kk_x = kk * jnp.exp(g - logw)
    r_d = r * jnp.exp(g)
    e_inv = jnp.exp(-g)
    k_d = k * e_inv
    b_d = b * e_inv
    e_tail = jnp.exp(g_tot - g)
    lhs = jnp.concatenate([kk_x, r_d], axis=0)
    a_k = _dot_nt(lhs, _stack_heads(k_d, blk))
    a_b = _dot_nt(lhs, _stack_heads(b_d, blk))
    a_kk_k = jnp.where(before, a_k[:c], 0.0)
    a_r_k = jnp.where(incl, a_k[c:], 0.0)
    n_mat = jnp.where(before, a_b[:c], 0.0)
    a_r_b = jnp.where(incl, a_b[c:], 0.0)
    p0 = _dot_nt(lhs, s_prev)
    av = _dot(jnp.concatenate([a_kk_k, a_r_k], axis=0), _stack_heads(v, blk))
    rhs = p0[:c] + av[:c]
    y1 = p0[c:] + av[c:]

    blk8 = (row // 8) == (col // 8)
    n0 = jnp.where(blk8, n_mat, 0.0)
    n2 = _dot(n0, _stack_heads(n0, blk))
    n4 = _dot(n2, _stack_heads(n2, blk))
    t = eye - n0
    t = _dot(t, _stack_heads(eye + n2, blk))
    t = _dot(t, _stack_heads(eye + n4, blk))
    s = 8
    while s < c:
        off = ((row // (2 * s)) == (col // (2 * s))) & ((row // s) != (col // s))
        x = _dot(t, _stack_heads(jnp.where(off, n_mat, 0.0), blk))
        t = t - _dot(x, _stack_heads(t, blk))
        s *= 2
    u = _dot(t, _stack_heads(rhs, blk))
    y = y1 - _dot(a_r_b, _stack_heads(u, blk))
    vu = jnp.concatenate([v, -u], axis=0)
    kb = jnp.concatenate([k * e_tail, b * e_tail], axis=0)
    s_new = s_prev * jnp.exp(g_tot) + jnp.where(blk, _dot_tn(vu, kb), 0.0)
    return y, s_new


def _wkv_kernel(n_chunks,
                rf, kkf, vf, lwf, kf, bf, rb, kkb, vb, lwb, kb, bb,
                yf_ref, yb_ref, sf_ref, sb_ref):
    @pl.when(pl.program_id(2) == 0)
    def _():
        sf_ref[...] = jnp.zeros_like(sf_ref)
        sb_ref[...] = jnp.zeros_like(sb_ref)

    def body(i, carry):
        fs = pl.ds(pl.multiple_of(i * CHUNK, CHUNK), CHUNK)
        bs = pl.ds(pl.multiple_of((n_chunks - 1 - i) * CHUNK, CHUNK), CHUNK)
        y, s_new = _wkv_chunk(rf[fs, :], kkf[fs, :], vf[fs, :], lwf[fs, :], kf[fs, :], bf[fs, :],
                              sf_ref[...], False)
        yf_ref[fs, :] = y
        sf_ref[...] = s_new
        y, s_new = _wkv_chunk(rb[bs, :], kkb[bs, :], vb[bs, :], lwb[bs, :], kb[bs, :], bb[bs, :],
                              sb_ref[...], True)
        yb_ref[bs, :] = y
        sb_ref[...] = s_new
        return carry

    lax.fori_loop(0, n_chunks, body, 0)


def _wkv(r, kk, v, lw_f, lw_b, k_f, k_b, b_f, b_b, batch, rows_per_step=512):
    n = r.shape[0]
    seq = n // batch
    ns = seq // rows_per_step
    ng = D_RWKV // GROUP_LANES
    blk = (rows_per_step, GROUP_LANES)
    fwd = pl.BlockSpec(blk, lambda bi, g, s: (bi * ns + s, g))
    bwd = pl.BlockSpec(blk, lambda bi, g, s: (bi * ns + ns - 1 - s, g))
    out_shape = jax.ShapeDtypeStruct((n, D_RWKV), F32)
    return pl.pallas_call(
        functools.partial(_wkv_kernel, rows_per_step // CHUNK),
        grid=(batch, ng, ns),
        in_specs=[fwd] * 6 + [bwd] * 6,
        out_specs=[fwd, bwd],
        out_shape=[out_shape, out_shape],
        scratch_shapes=[pltpu.VMEM((GROUP_LANES, GROUP_LANES), F32)] * 2,
        compiler_params=_params(("parallel", "parallel", "arbitrary")),
        name="wkv7",
    )(r, kk, v, lw_f, k_f, b_f, r, kk, v, lw_b, k_b, b_b)


def _post_kernel(yf_ref, yb_ref, g_ref, bonus_ref, lg_ref, lb_ref, o_ref):
    ones = _head_ones(GROUP_LANES).astype(BF16)
    y = yf_ref[...] + yb_ref[...]
    mu = _head_sum(y, ones) * (1.0 / HEAD_DIM)
    yc = y - mu
    var = _head_sum(yc * yc, ones) * (1.0 / HEAD_DIM)
    yn = yc * lax.rsqrt(var + LNX_EPS) * lg_ref[...] + lb_ref[...]
    o_ref[...] = ((yn + bonus_ref[...]) * g_ref[...]).astype(o_ref.dtype)


def _rwkv_post(yf, yb, g, bonus, lnx_g, lnx_b, tm=512):
    n = yf.shape[0]
    ng = D_RWKV // GROUP_LANES
    big = pl.BlockSpec((tm, GROUP_LANES), lambda i, j: (i, j))
    vec = pl.BlockSpec((1, GROUP_LANES), lambda i, j: (0, j))
    return pl.pallas_call(
        _post_kernel,
        grid=(n // tm, ng),
        in_specs=[big] * 4 + [vec] * 2,
        out_specs=big,
        out_shape=jax.ShapeDtypeStruct((n, D_RWKV), BF16),
        compiler_params=_params(("parallel", "parallel")),
        name="rwkv_post",
    )(yf, yb, g, bonus, lnx_g, lnx_b)


POOL_PAD_ROWS = 128


def _pool_kernel(tiles_per_seq, p_ref, prev_ref, next_ref, w_ref, scale_ref, o_ref):
    i = pl.program_id(0)
    gi = pl.program_id(1)
    p = p_ref[...]
    tm = p.shape[0]
    ext = jnp.concatenate(
        [prev_ref[...], p, next_ref[...], jnp.zeros((POOL_PAD_ROWS - 2 * HALO, p.shape[1]), F32)], axis=0)
    ke = tm + POOL_PAD_ROWS
    half = jnp.left_shift(1, gi)
    t = lax.broadcasted_iota(jnp.int32, (tm, ke), 0)
    s = lax.broadcasted_iota(jnp.int32, (tm, ke), 1) - HALO
    pos = (i % tiles_per_seq) * tm + s
    seq = tiles_per_seq * tm
    band = (s >= t - half) & (s < t + half) & (pos >= 0) & (pos < seq) & (s < tm + HALO)
    band = jnp.where(band, 1.0, 0.0).astype(BF16)
    hi, lo = _split2(ext)
    tot = jnp.dot(band, hi, preferred_element_type=F32) + jnp.dot(band, lo, preferred_element_type=F32)
    tpos = (i % tiles_per_seq) * tm + lax.broadcasted_iota(jnp.int32, (tm, 1), 0)
    cnt = jnp.minimum(tpos + half, seq) - jnp.maximum(tpos - half, 0)
    z = tot / cnt.astype(F32) - p
    o_ref[...] = (_dot(z, w_ref[0]) * scale_ref[...]).astype(o_ref.dtype)


def _pool(p_pool, pool_w, pool_scale, seq_len, tm=256):
    n = p_pool.shape[0]
    ng = len(POOL_WINDOWS)
    tph = tm // HALO
    nh = n // HALO
    return pl.pallas_call(
        functools.partial(_pool_kernel, seq_len // tm),
        grid=(n // tm, ng),
        in_specs=[pl.BlockSpec((tm, POOL_GROUP), lambda i, g: (i, g)),
                  pl.BlockSpec((HALO, POOL_GROUP), lambda i, g: (jnp.maximum(i * tph - 1, 0), g)),
                  pl.BlockSpec((HALO, POOL_GROUP), lambda i, g: (jnp.minimum((i + 1) * tph, nh - 1), g)),
                  pl.BlockSpec((1, POOL_GROUP, POOL_GROUP), lambda i, g: (g, 0, 0)),
                  pl.BlockSpec((1, POOL_GROUP), lambda i, g: (0, g))],
        out_specs=pl.BlockSpec((tm, POOL_GROUP), lambda i, g: (i, g)),
        out_shape=jax.ShapeDtypeStruct((n, ng * POOL_GROUP), BF16),
        compiler_params=_params(("parallel", "parallel")),
        name="pool",
    )(p_pool, p_pool, p_pool, pool_w, pool_scale)


def _merge_kernel(u_ref, ya_ref, yb_ref, wga_ref, wgb_ref, wua_ref, wub_ref, o_ref):
    u = u_ref[...]
    ga = jnp.dot(u, wga_ref[...], preferred_element_type=F32)
    gb = jnp.dot(u, wgb_ref[...], preferred_element_type=F32)
    ua = jnp.dot(ya_ref[...], wua_ref[...], preferred_element_type=F32)
    ub = jnp.dot(yb_ref[...], wub_ref[...], preferred_element_type=F32)
    o_ref[...] = (_sigmoid(ga) * ua + _sigmoid(gb) * ub).astype(o_ref.dtype)


def _merge(u, ya, yb, wga, wgb, wua, wub, tm=512, tn=512):
    n, d = u.shape
    dn = wga.shape[1]
    row = lambda kdim: pl.BlockSpec((tm, kdim), lambda j, i: (i, 0))
    col = lambda kdim: pl.BlockSpec((kdim, tn), lambda j, i: (0, j))
    return pl.pallas_call(
        _merge_kernel,
        grid=(dn // tn, n // tm),
        in_specs=[row(d), row(ya.shape[1]), row(yb.shape[1]), col(d), col(d), col(ya.shape[1]), col(yb.shape[1])],
        out_specs=pl.BlockSpec((tm, tn), lambda j, i: (i, j)),
        out_shape=jax.ShapeDtypeStruct((n, dn), BF16),
        compiler_params=_params(("parallel", "parallel")),
        name="merge",
    )(u, ya, yb, wga, wgb, wua, wub)


def _router_kernel(x_ref, g_ref, wr_ref, v_ref, aff_ref):
    x = x_ref[...]
    ms = jnp.mean(x * x, axis=-1, keepdims=True)
    v = x * lax.rsqrt(ms + NORM_EPS) * g_ref[...]
    v_ref[...] = v
    vh, vl = _split2(v)
    wh, wl = _split2(wr_ref[...])
    logits = (jnp.dot(vh, wh, preferred_element_type=F32) + jnp.dot(vl, wh, preferred_element_type=F32)
              + jnp.dot(vh, wl, preferred_element_type=F32))
    m = jnp.max(logits, axis=-1, keepdims=True)
    e = jnp.exp(logits - m)
    aff_ref[...] = e / jnp.sum(e, axis=-1, keepdims=True)


def _router(x, g, w_router, tm=512):
    n, d = x.shape
    ne = w_router.shape[1]
    return pl.pallas_call(
        _router_kernel,
        grid=(n // tm,),
        in_specs=[pl.BlockSpec((tm, d), lambda i: (i, 0)), pl.BlockSpec((1, d), lambda i: (0, 0)),
                  pl.BlockSpec((d, ne), lambda i: (0, 0))],
        out_specs=[pl.BlockSpec((tm, d), lambda i: (i, 0)), pl.BlockSpec((tm, ne), lambda i: (i, 0))],
        out_shape=[jax.ShapeDtypeStruct((n, d), F32), jax.ShapeDtypeStruct((n, ne), F32)],
        compiler_params=_params(("parallel",)),
        name="router",
    )(x, g.reshape(1, d), w_router)


def _moe_kernel(idx_ref, gates_ref, wg_ref, wu_ref, wd_ref, vn_hbm, x_hbm, xo_hbm,
                rows, xs, acc, sem_in, sem_out):
    del x_hbm
    e = pl.program_id(0)
    f = pl.program_id(1)
    nf = pl.num_programs(1)
    n_rows = rows.shape[0]

    def gather(src_hbm):
        def start(i, c):
            pltpu.make_async_copy(src_hbm.at[pl.ds(idx_ref[e, i], 1)], rows.at[pl.ds(i, 1)], sem_in).start()
            return c
        lax.fori_loop(0, n_rows, start, 0)

    def gather_wait(src_hbm):
        def wait(i, c):
            pltpu.make_async_copy(src_hbm.at[pl.ds(0, 1)], rows.at[pl.ds(i, 1)], sem_in).wait()
            return c
        lax.fori_loop(0, n_rows, wait, 0)

    @pl.when(f == 0)
    def _():
        gather(vn_hbm)
        gather_wait(vn_hbm)
        xs[...] = rows[...].astype(BF16)
        gather(xo_hbm)
        acc[...] = jnp.zeros_like(acc)

    x = xs[...]
    hg = jnp.dot(x, wg_ref[0].astype(BF16), preferred_element_type=F32)
    hu = jnp.dot(x, wu_ref[0].astype(BF16), preferred_element_type=F32)
    h = (hg * _sigmoid(hg) * hu).astype(BF16)
    acc[...] += jnp.dot(h, wd_ref[0].astype(BF16), preferred_element_type=F32)

    @pl.when(f == nf - 1)
    def _():
        gather_wait(xo_hbm)
        rows[...] = rows[...] + acc[...] * gates_ref[0]

        def start(i, c):
            pltpu.make_async_copy(rows.at[pl.ds(i, 1)], xo_hbm.at[pl.ds(idx_ref[e, i], 1)], sem_out).start()
            return c
        lax.fori_loop(0, n_rows, start, 0)

        def wait(i, c):
            pltpu.make_async_copy(rows.at[pl.ds(i, 1)], xo_hbm.at[pl.ds(0, 1)], sem_out).wait()
            return c
        lax.fori_loop(0, n_rows, wait, 0)


def _moe(x, vn, idx, gates, w_gate, w_up, w_down, tf=256):
    n, d = x.shape
    ne, n_rows = idx.shape
    ff = w_gate.shape[2]
    grid_spec = pltpu.PrefetchScalarGridSpec(
        num_scalar_prefetch=1,
        grid=(ne, ff // tf),
        in_specs=[pl.BlockSpec((1, n_rows, 1), lambda e, f, idx: (e, 0, 0)),
                  pl.BlockSpec((1, d, tf), lambda e, f, idx: (e, 0, f)),
                  pl.BlockSpec((1, d, tf), lambda e, f, idx: (e, 0, f)),
                  pl.BlockSpec((1, tf, d), lambda e, f, idx: (e, f, 0)),
                  pl.BlockSpec(memory_space=pl.ANY),
                  pl.BlockSpec(memory_space=pl.ANY)],
        out_specs=pl.BlockSpec(memory_space=pl.ANY),
        scratch_shapes=[pltpu.VMEM((n_rows, d), F32), pltpu.VMEM((n_rows, d), BF16),
                        pltpu.VMEM((n_rows, d), F32), pltpu.SemaphoreType.DMA, pltpu.SemaphoreType.DMA],
    )
    return pl.pallas_call(
        _moe_kernel,
        grid_spec=grid_spec,
        out_shape=jax.ShapeDtypeStruct((n, d), F32),
        input_output_aliases={6: 0},
        compiler_params=_params(("arbitrary", "arbitrary")),
        name="moe",
    )(idx, gates, w_gate, w_up, w_down, vn, x)


def _pad_rows(w, rows):
    return jnp.pad(w, [(0, 0)] * (w.ndim - 2) + [(0, rows - w.shape[-2]), (0, 0)])


def _pad_cols(w, cols):
    return jnp.pad(w, [(0, 0)] * (w.ndim - 1) + [(0, cols - w.shape[-1])])


def kernel(x, norm_mix_g, w_in, mu_shift, w0, w_lora_up, a0, a_lora_up, g_lora_up, k_k, k_a, r_k,
           lnx_g, lnx_b, pool_w, pool_scale, w_up_a, w_up_b, w_o, norm_moe_g, w_router,
           w_gate_e, w_up_e, w_down_e, final_g):
    batch, seq, d = x.shape
    n = batch * seq
    depth = w_in.shape[0]
    d_pool = pool_w.shape[1] * pool_w.shape[2]
    lora = w_lora_up.shape[2]
    gate_lora = g_lora_up.shape[1]
    c_lora = 3 * D_RWKV
    c_pool = c_lora + 4 * lora + gate_lora
    c_ga = c_pool + d_pool
    c_gb = c_ga + d
    cap = CAPACITY_FACTOR * seq // N_EXPERTS

    def lora_cols(w):
        segs = [_pad_cols(w[..., c_lora + j * lora:c_lora + (j + 1) * lora], LORA_PAD) for j in range(4)]
        segs.append(_pad_cols(w[..., c_lora + 4 * lora:c_pool], GATE_PAD))
        return jnp.concatenate(segs, axis=-1)

    xf = x.reshape(n, d)
    for l in range(depth):
        wl = w_in[l]
        u = _rmsnorm(xf, norm_mix_g[l], BF16)
        p_rkv = _matmul(u, wl[:, :c_lora].astype(BF16), tn=1024, name="proj_rkv")
        p_lora = _matmul(u, lora_cols(wl).astype(BF16), tn=4 * LORA_PAD + GATE_PAD, name="proj_lora")
        p_pool = _matmul(u, wl[:, c_pool:c_ga].astype(BF16), tn=1024, name="proj_pool")
        mu = mu_shift[l]
        prep = _rwkv_prep(
            p_rkv, p_lora, mu[:c_lora].reshape(1, -1), lora_cols(mu).reshape(1, -1),
            w0[l], _pad_rows(w_lora_up[l], LORA_PAD).astype(BF16),
            a0[l], _pad_rows(a_lora_up[l], LORA_PAD).astype(BF16),
            _pad_rows(g_lora_up[l], GATE_PAD).astype(BF16),
            k_k[l].reshape(1, -1), k_a[l].reshape(1, -1), r_k[l].reshape(1, -1), seq)
        r, kk, v, lw_f, lw_b, k_f, k_b, b_f, b_b, g, bonus = prep
        yf, yb = _wkv(r, kk, v, lw_f, lw_b, k_f, k_b, b_f, b_b, batch)
        y_a = _rwkv_post(yf, yb, g, bonus, lnx_g[l].reshape(1, -1), lnx_b[l].reshape(1, -1))
        y_b = _pool(p_pool, pool_w[l].astype(BF16), pool_scale[l].reshape(1, -1), seq)
        merged = _merge(u, y_a, y_b, wl[:, c_ga:c_gb].astype(BF16), wl[:, c_gb:].astype(BF16),
                        w_up_a[l].astype(BF16), w_up_b[l].astype(BF16))
        xf = _matmul(merged, w_o[l].astype(BF16), residual=xf, name="out_proj")

        vn, aff = _router(xf, norm_moe_g[l], w_router[l])
        aff_t = jnp.swapaxes(aff.reshape(batch, seq, N_EXPERTS), 1, 2)
        gates, idx = lax.top_k(aff_t, cap)
        rows = idx + (jnp.arange(batch, dtype=idx.dtype) * seq)[:, None, None]
        rows = jnp.swapaxes(rows, 0, 1).reshape(N_EXPERTS, batch * cap).astype(jnp.int32)
        gates = jnp.swapaxes(gates, 0, 1).reshape(N_EXPERTS, batch * cap, 1)
        xf = _moe(xf, vn, rows, gates, w_gate_e[l], w_up_e[l], w_down_e[l])
    out = _rmsnorm(xf, final_g, F32)
    return out.reshape(batch, seq, d)
```

```python
import functools

import numpy as np
import jax
import jax.numpy as jnp
from jax import lax
from jax.experimental import pallas as pl
from jax.experimental.pallas import tpu as pltpu

F32 = jnp.float32
BF16 = jnp.bfloat16

HEAD_DIM = 64
N_HEADS = 16
D_RWKV = N_HEADS * HEAD_DIM
LORA_PAD = 128
GATE_PAD = 256
POOL_WINDOWS = (2, 4, 8, 16)
POOL_GROUP = 256
N_EXPERTS = 16
CAPACITY_FACTOR = 2
NORM_EPS = 1e-6
LNX_EPS = 64e-5

CHUNK = 64
GROUP_LANES = 256
HEADS_PER_GROUP = GROUP_LANES // HEAD_DIM
HALO = 8
PREP_ROWS = 256
DMA_UNROLL = 8
VMEM_LIMIT = 56 * 1024 * 1024


def _params(sem):
    return pltpu.CompilerParams(dimension_semantics=sem, vmem_limit_bytes=VMEM_LIMIT)


def _dot(a, b):
    return jnp.dot(a.astype(BF16), b.astype(BF16), preferred_element_type=F32)


def _split2(x):
    hi = x.astype(BF16)
    lo = (x - hi.astype(F32)).astype(BF16)
    return hi, lo


def _split3(x):
    hi = x.astype(BF16)
    r1 = x - hi.astype(F32)
    mid = r1.astype(BF16)
    lo = (r1 - mid.astype(F32)).astype(BF16)
    return hi, mid, lo


def _head_ones(n):
    r = lax.broadcasted_iota(jnp.int32, (n, n), 0) // HEAD_DIM
    c = lax.broadcasted_iota(jnp.int32, (n, n), 1) // HEAD_DIM
    return r == c


def _head_sum(x, ones_bf16):
    hi, lo = _split2(x)
    return (jnp.dot(hi, ones_bf16, preferred_element_type=F32)
            + jnp.dot(lo, ones_bf16, preferred_element_type=F32))


def _rmsnorm_kernel(x_ref, g_ref, o_ref):
    x = x_ref[...]
    ms = jnp.mean(x * x, axis=-1, keepdims=True)
    o_ref[...] = (x * lax.rsqrt(ms + NORM_EPS) * g_ref[...]).astype(o_ref.dtype)


def _rmsnorm(x, g, out_dtype, tm=512):
    n, d = x.shape
    return pl.pallas_call(
        _rmsnorm_kernel,
        grid=(n // tm,),
        in_specs=[pl.BlockSpec((tm, d), lambda i: (i, 0)), pl.BlockSpec((1, d), lambda i: (0, 0))],
        out_specs=pl.BlockSpec((tm, d), lambda i: (i, 0)),
        out_shape=jax.ShapeDtypeStruct((n, d), out_dtype),
        compiler_params=_params(("parallel",)),
        name="rmsnorm",
    )(x, g.reshape(1, d))


def _mm_kernel(a_ref, w_ref, o_ref):
    o_ref[...] = jnp.dot(a_ref[...], w_ref[...], preferred_element_type=F32).astype(o_ref.dtype)


def _mm_res_kernel(a_ref, w_ref, r_ref, o_ref):
    o_ref[...] = r_ref[...] + jnp.dot(a_ref[...], w_ref[...], preferred_element_type=F32)


def _matmul(a, w, out_dtype=F32, residual=None, tm=512, tn=512, name="matmul"):
    m, k = a.shape
    nn = w.shape[1]
    tn = min(tn, nn)
    in_specs = [pl.BlockSpec((tm, k), lambda j, i: (i, 0)), pl.BlockSpec((k, tn), lambda j, i: (0, j))]
    args = [a, w]
    kern = _mm_kernel
    if residual is not None:
        in_specs.append(pl.BlockSpec((tm, tn), lambda j, i: (i, j)))
        args.append(residual)
        kern = _mm_res_kernel
    return pl.pallas_call(
        kern,
        grid=(nn // tn, m // tm),
        in_specs=in_specs,
        out_specs=pl.BlockSpec((tm, tn), lambda j, i: (i, j)),
        out_shape=jax.ShapeDtypeStruct((m, nn), out_dtype),
        compiler_params=_params(("parallel", "parallel")),
        name=name,
    )(*args)


def _token_shift(p_ref, prev_ref, next_ref, mu, first, last):
    p = p_ref[...]
    tm = p.shape[0]
    prev_row = jnp.where(first, 0.0, prev_ref[HALO - 1:HALO, :])
    next_row = jnp.where(last, 0.0, next_ref[0:1, :])
    rows = lax.broadcasted_iota(jnp.int32, p.shape, 0)
    p_prev = jnp.where(rows == 0, prev_row, pltpu.roll(p, 1, 0))
    p_next = jnp.where(rows == tm - 1, next_row, pltpu.roll(p, tm - 1, 0))
    return p + mu * (0.5 * (p_prev + p_next) - p)


def _softplus(x):
    return jnp.maximum(x, 0.0) + jnp.log(1.0 + jnp.exp(-jnp.abs(x)))


def _sigmoid(x):
    return 1.0 / (1.0 + jnp.exp(-x))


def _prep_kernel(tiles_per_seq,
                 r_ref, rp_ref, rn_ref, k_ref, kp_ref, kn_ref, v_ref, vp_ref, vn_ref,
                 lo_ref, lop_ref, lon_ref,
                 mur_ref, muk_ref, muv_ref, mulo_ref,
                 w0_ref, wup_ref, a0_ref, aup_ref, gup_ref, kk_ref, ka_ref, rk_ref, tri_ref,
                 o_r, o_kk, o_v, o_lwf, o_lwb, o_gf, o_gb, o_kf, o_kb, o_bf, o_bb, o_g, o_bonus):
    i = pl.program_id(0)
    first = (i % tiles_per_seq) == 0
    last = (i % tiles_per_seq) == tiles_per_seq - 1
    r = _token_shift(r_ref, rp_ref, rn_ref, mur_ref[...], first, last)
    k = _token_shift(k_ref, kp_ref, kn_ref, muk_ref[...], first, last)
    v = _token_shift(v_ref, vp_ref, vn_ref, muv_ref[...], first, last)
    lo = _token_shift(lo_ref, lop_ref, lon_ref, mulo_ref[...], first, last)
    ones = _head_ones(GROUP_LANES).astype(BF16)

    kk0 = k * kk_ref[...]
    nrm = jnp.sqrt(_head_sum(kk0 * kk0, ones))
    kk = kk0 / jnp.maximum(nrm, 1e-12)
    o_r[...] = r
    o_kk[...] = kk
    o_v[...] = v

    k_dirs = []
    for d, (o_lw, o_gc, o_k, o_b) in enumerate(((o_lwf, o_gf, o_kf, o_bf), (o_lwb, o_gb, o_kb, o_bb))):
        wd = lo[:, d * LORA_PAD:(d + 1) * LORA_PAD]
        ad = lo[:, (2 + d) * LORA_PAD:(3 + d) * LORA_PAD]
        w_pre = w0_ref[d:d + 1, :] + _dot(jnp.tanh(wd), wup_ref[d])
        lw = -jnp.exp(-_softplus(-w_pre) - 0.5)
        o_lw[...] = lw
        o_gc[...] = sum(jnp.dot(tri_ref[d], part, preferred_element_type=F32) for part in _split3(lw))
        a = _sigmoid(a0_ref[d:d + 1, :] + _dot(ad, aup_ref[d]))
        k_d = k * (1.0 + (a - 1.0) * ka_ref[...])
        o_k[...] = k_d
        o_b[...] = kk * a
        k_dirs.append(k_d)
    gd = lo[:, 4 * LORA_PAD:4 * LORA_PAD + GATE_PAD]
    o_g[...] = _dot(_sigmoid(gd), gup_ref[...])
    k_bonus = 0.5 * (k_dirs[0] + k_dirs[1])
    o_bonus[...] = _head_sum(r * k_bonus * rk_ref[...], ones) * v


def _chunk_cumsum_matrices(tm):
    t = np.arange(tm)[:, None]
    s = np.arange(tm)[None, :]
    same_chunk = (t // CHUNK) == (s // CHUNK)
    return jnp.asarray(np.stack([same_chunk & (s <= t), same_chunk & (s >= t)]), BF16)


def _rwkv_prep(p_rkv, p_lora, mu_rkv, mu_lora, w0, wup, a0, aup, gup, k_k, k_a, r_k, seq_len, tm=PREP_ROWS):
    n = p_rkv.shape[0]
    ng = D_RWKV // GROUP_LANES
    nlo = p_lora.shape[1]
    tph = tm // HALO
    nh = n // HALO

    def main(col0):
        return pl.BlockSpec((tm, GROUP_LANES), lambda i, g: (i, col0 + g))

    def prev(col0):
        return pl.BlockSpec((HALO, GROUP_LANES), lambda i, g: (jnp.maximum(i * tph - 1, 0), col0 + g))

    def nxt(col0):
        return pl.BlockSpec((HALO, GROUP_LANES), lambda i, g: (jnp.minimum((i + 1) * tph, nh - 1), col0 + g))

    in_specs = []
    args = []
    for c in range(3):
        in_specs += [main(c * ng), prev(c * ng), nxt(c * ng)]
        args += [p_rkv, p_rkv, p_rkv]
    in_specs += [pl.BlockSpec((tm, nlo), lambda i, g: (i, 0)),
                 pl.BlockSpec((HALO, nlo), lambda i, g: (jnp.maximum(i * tph - 1, 0), 0)),
                 pl.BlockSpec((HALO, nlo), lambda i, g: (jnp.minimum((i + 1) * tph, nh - 1), 0))]
    args += [p_lora, p_lora, p_lora]
    for c in range(3):
        in_specs.append(pl.BlockSpec((1, GROUP_LANES), lambda i, g, c=c: (0, c * ng + g)))
        args.append(mu_rkv)
    in_specs.append(pl.BlockSpec((1, nlo), lambda i, g: (0, 0)))
    args.append(mu_lora)
    vec2 = pl.BlockSpec((2, GROUP_LANES), lambda i, g: (0, g))
    lora2 = pl.BlockSpec((2, LORA_PAD, GROUP_LANES), lambda i, g: (0, 0, g))
    vec1 = pl.BlockSpec((1, GROUP_LANES), lambda i, g: (0, g))
    in_specs += [vec2, lora2, vec2, lora2, pl.BlockSpec((GATE_PAD, GROUP_LANES), lambda i, g: (0, g)),
                 vec1, vec1, vec1, pl.BlockSpec((2, tm, tm), lambda i, g: (0, 0, 0))]
    args += [w0, wup, a0, aup, gup, k_k, k_a, r_k, _chunk_cumsum_matrices(tm)]
    out_spec = pl.BlockSpec((tm, GROUP_LANES), lambda i, g: (i, g))
    out_shape = jax.ShapeDtypeStruct((n, D_RWKV), F32)
    return pl.pallas_call(
        functools.partial(_prep_kernel, seq_len // tm),
        grid=(n // tm, ng),
        in_specs=in_specs,
        out_specs=[out_spec] * 13,
        out_shape=[out_shape] * 13,
        compiler_params=_params(("parallel", "parallel")),
        name="rwkv_prep",
    )(*args)


M_BEFORE_F, M_INCL_F, M_BEFORE_B, M_INCL_B, M_EYE, M_BLK8, M_OFF = 0, 1, 2, 3, 4, 5, 6


def _wkv_consts():
    c = CHUNK
    row = np.arange(c)[:, None]
    col = (np.arange(GROUP_LANES) % c)[None, :]
    same = col == row
    masks = [col < row, (col < row) | same, col > row, (col > row) | same, same, (row // 8) == (col // 8)]
    s = 8
    while s < c:
        masks.append(((row // (2 * s)) == (col // (2 * s))) & ((row // s) != (col // s)))
        s *= 2
    head = np.arange(GROUP_LANES) // HEAD_DIM
    blk = head[:, None] == head[None, :]
    return jnp.asarray(np.stack(masks), F32), jnp.asarray(blk, BF16)


def _wkv_chunks(streams, m_ref, blk_ref):
    c = CHUNK
    ns = range(len(streams))
    r, kk, v, lw, g, k, b, s_prev, rev = (list(col) for col in zip(*streams))
    before = [m_ref[M_BEFORE_B if rev[i] else M_BEFORE_F] for i in ns]
    incl = [m_ref[M_INCL_B if rev[i] else M_INCL_F] for i in ns]
    eye = m_ref[M_EYE]

    def stack(x):
        return jnp.concatenate([x.astype(BF16)] * HEADS_PER_GROUP, axis=0) * blk_ref[...]

    def dot(a, bm):
        return jnp.dot(a.astype(BF16), bm, preferred_element_type=F32)

    def dot_nt(a, bm):
        return lax.dot_general(a, bm, (((1,), (1,)), ((), ())), preferred_element_type=F32)

    g_tot = [g[i][0:1, :] if rev[i] else g[i][c - 1:c, :] for i in ns]
    e_inv = [jnp.exp(-g[i]) for i in ns]
    lhs = [jnp.concatenate([kk[i] * jnp.exp(g[i] - lw[i]), r[i] * jnp.exp(g[i])], axis=0).astype(BF16)
           for i in ns]
    a_k = [dot_nt(lhs[i], stack(k[i] * e_inv[i])) for i in ns]
    a_b = [dot_nt(lhs[i], stack(b[i] * e_inv[i])) for i in ns]
    p0 = [dot_nt(lhs[i], s_prev[i].astype(BF16)) for i in ns]
    n_mat = [a_b[i][:c] * before[i] for i in ns]
    av = [dot(jnp.concatenate([a_k[i][:c] * before[i], a_k[i][c:] * incl[i]], axis=0), stack(v[i])) for i in ns]

    n0 = [n_mat[i] * m_ref[M_BLK8] for i in ns]
    n2 = [dot(n0[i], stack(n0[i])) for i in ns]
    n4 = [dot(n2[i], stack(n2[i])) for i in ns]
    t = [dot(eye - n0[i], stack(eye + n2[i])) for i in ns]
    t = [dot(t[i], stack(eye + n4[i])) for i in ns]
    for lvl in range(M_OFF, m_ref.shape[0]):
        x = [dot(t[i], stack(n_mat[i] * m_ref[lvl])) for i in ns]
        t = [t[i] - dot(x[i], stack(t[i])) for i in ns]
    u = [dot(t[i], stack(p0[i][:c] + av[i][:c])) for i in ns]
    y = [p0[i][c:] + av[i][c:] - dot(a_b[i][c:] * incl[i], stack(u[i])) for i in ns]
    out = []
    for i in ns:
        e_tail = jnp.exp(g_tot[i] - g[i])
        vu = jnp.concatenate([v[i], -u[i]], axis=0).astype(BF16)
        kb = jnp.concatenate([k[i] * e_tail, b[i] * e_tail], axis=0).astype(BF16)
        ds = lax.dot_general(vu, kb, (((0,), (0,)), ((), ())), preferred_element_type=F32)
        out.append((y[i], s_prev[i] * jnp.exp(g_tot[i]) + ds * blk_ref[...].astype(F32)))
    return out


def _wkv_kernel(n_chunks, m_ref, blk_ref,
                rf, kkf, vf, lwf, gf, kf, bf, rb, kkb, vb, lwb, gb, kb, bb,
                yf_ref, yb_ref, sf_ref, sb_ref):
    @pl.when(pl.program_id(1) == 0)
    def _():
        sf_ref[...] = jnp.zeros_like(sf_ref)
        sb_ref[...] = jnp.zeros_like(sb_ref)

    for ci in range(n_chunks):
        fs = slice(ci * CHUNK, (ci + 1) * CHUNK)
        bs = slice((n_chunks - 1 - ci) * CHUNK, (n_chunks - ci) * CHUNK)
        streams = []
        for gi in range(sf_ref.shape[0]):
            ls = slice(gi * GROUP_LANES, (gi + 1) * GROUP_LANES)
            streams.append((rf[fs, ls], kkf[fs, ls], vf[fs, ls], lwf[fs, ls], gf[fs, ls], kf[fs, ls],
                            bf[fs, ls], sf_ref[gi], False))
            streams.append((rb[bs, ls], kkb[bs, ls], vb[bs, ls], lwb[bs, ls], gb[bs, ls], kb[bs, ls],
                            bb[bs, ls], sb_ref[gi], True))
        results = _wkv_chunks(streams, m_ref, blk_ref)
        for gi in range(sf_ref.shape[0]):
            ls = slice(gi * GROUP_LANES, (gi + 1) * GROUP_LANES)
            (y_f, s_f), (y_b, s_b) = results[2 * gi], results[2 * gi + 1]
            yf_ref[fs, ls] = y_f
            sf_ref[gi] = s_f
            yb_ref[bs, ls] = y_b
            sb_ref[gi] = s_b


def _wkv(r, kk, v, lw_f, lw_b, g_f, g_b, k_f, k_b, b_f, b_b, batch, rows_per_step=CHUNK):
    n = r.shape[0]
    seq = n // batch
    ns = seq // rows_per_step
    ng = D_RWKV // GROUP_LANES
    masks, blk = _wkv_consts()
    fwd = pl.BlockSpec((rows_per_step, D_RWKV), lambda bi, s: (bi * ns + s, 0))
    bwd = pl.BlockSpec((rows_per_step, D_RWKV), lambda bi, s: (bi * ns + ns - 1 - s, 0))
    out_shape = jax.ShapeDtypeStruct((n, D_RWKV), F32)
    state = pltpu.VMEM((ng, GROUP_LANES, GROUP_LANES), F32)
    return pl.pallas_call(
        functools.partial(_wkv_kernel, rows_per_step // CHUNK),
        grid=(batch, ns),
        in_specs=[pl.BlockSpec(masks.shape, lambda bi, s: (0, 0, 0)), pl.BlockSpec(blk.shape, lambda bi, s: (0, 0))]
        + [fwd] * 7 + [bwd] * 7,
        out_specs=[fwd, bwd],
        out_shape=[out_shape, out_shape],
        scratch_shapes=[state, state],
        compiler_params=_params(("parallel", "arbitrary")),
        name="wkv7",
    )(masks, blk, r, kk, v, lw_f, g_f, k_f, b_f, r, kk, v, lw_b, g_b, k_b, b_b)


def _post_kernel(yf_ref, yb_ref, g_ref, bonus_ref, lg_ref, lb_ref, o_ref):
    ones = _head_ones(GROUP_LANES).astype(BF16)
    y = yf_ref[...] + yb_ref[...]
    mu = _head_sum(y, ones) * (1.0 / HEAD_DIM)
    yc = y - mu
    var = _head_sum(yc * yc, ones) * (1.0 / HEAD_DIM)
    yn = yc * lax.rsqrt(var + LNX_EPS) * lg_ref[...] + lb_ref[...]
    o_ref[...] = ((yn + bonus_ref[...]) * g_ref[...]).astype(o_ref.dtype)


def _rwkv_post(yf, yb, g, bonus, lnx_g, lnx_b, tm=512):
    n = yf.shape[0]
    ng = D_RWKV // GROUP_LANES
    big = pl.BlockSpec((tm, GROUP_LANES), lambda i, j: (i, j))
    vec = pl.BlockSpec((1, GROUP_LANES), lambda i, j: (0, j))
    return pl.pallas_call(
        _post_kernel,
        grid=(n // tm, ng),
        in_specs=[big] * 4 + [vec] * 2,
        out_specs=big,
        out_shape=jax.ShapeDtypeStruct((n, D_RWKV), BF16),
        compiler_params=_params(("parallel", "parallel")),
        name="rwkv_post",
    )(yf, yb, g, bonus, lnx_g, lnx_b)


POOL_PAD_ROWS = 128


def _pool_kernel(tiles_per_seq, p_ref, prev_ref, next_ref, w_ref, scale_ref, o_ref):
    i = pl.program_id(0)
    gi = pl.program_id(1)
    p = p_ref[...]
    tm = p.shape[0]
    ext = jnp.concatenate(
        [prev_ref[...], p, next_ref[...], jnp.zeros((POOL_PAD_ROWS - 2 * HALO, p.shape[1]), F32)], axis=0)
    ke = tm + POOL_PAD_ROWS
    half = jnp.left_shift(1, gi)
    t = lax.broadcasted_iota(jnp.int32, (tm, ke), 0)
    s = lax.broadcasted_iota(jnp.int32, (tm, ke), 1) - HALO
    pos = (i % tiles_per_seq) * tm + s
    seq = tiles_per_seq * tm
    band = (s >= t - half) & (s < t + half) & (pos >= 0) & (pos < seq) & (s < tm + HALO)
    band = jnp.where(band, 1.0, 0.0).astype(BF16)
    hi, lo = _split2(ext)
    tot = jnp.dot(band, hi, preferred_element_type=F32) + jnp.dot(band, lo, preferred_element_type=F32)
    tpos = (i % tiles_per_seq) * tm + lax.broadcasted_iota(jnp.int32, (tm, 1), 0)
    cnt = jnp.minimum(tpos + half, seq) - jnp.maximum(tpos - half, 0)
    z = tot / cnt.astype(F32) - p
    o_ref[...] = (_dot(z, w_ref[0]) * scale_ref[...]).astype(o_ref.dtype)


def _pool(p_pool, pool_w, pool_scale, seq_len, tm=256):
    n = p_pool.shape[0]
    ng = len(POOL_WINDOWS)
    tph = tm // HALO
    nh = n // HALO
    return pl.pallas_call(
        functools.partial(_pool_kernel, seq_len // tm),
        grid=(n // tm, ng),
        in_specs=[pl.BlockSpec((tm, POOL_GROUP), lambda i, g: (i, g)),
                  pl.BlockSpec((HALO, POOL_GROUP), lambda i, g: (jnp.maximum(i * tph - 1, 0), g)),
                  pl.BlockSpec((HALO, POOL_GROUP), lambda i, g: (jnp.minimum((i + 1) * tph, nh - 1), g)),
                  pl.BlockSpec((1, POOL_GROUP, POOL_GROUP), lambda i, g: (g, 0, 0)),
                  pl.BlockSpec((1, POOL_GROUP), lambda i, g: (0, g))],
        out_specs=pl.BlockSpec((tm, POOL_GROUP), lambda i, g: (i, g)),
        out_shape=jax.ShapeDtypeStruct((n, ng * POOL_GROUP), BF16),
        compiler_params=_params(("parallel", "parallel")),
        name="pool",
    )(p_pool, p_pool, p_pool, pool_w, pool_scale)


def _merge_kernel(u_ref, ya_ref, yb_ref, wga_ref, wgb_ref, wua_ref, wub_ref, o_ref):
    u = u_ref[...]
    ga = jnp.dot(u, wga_ref[...], preferred_element_type=F32)
    gb = jnp.dot(u, wgb_ref[...], preferred_element_type=F32)
    ua = jnp.dot(ya_ref[...], wua_ref[...], preferred_element_type=F32)
    ub = jnp.dot(yb_ref[...], wub_ref[...], preferred_element_type=F32)
    o_ref[...] = (_sigmoid(ga) * ua + _sigmoid(gb) * ub).astype(o_ref.dtype)


def _merge(u, ya, yb, wga, wgb, wua, wub, tm=512, tn=512):
    n, d = u.shape
    dn = wga.shape[1]
    row = lambda kdim: pl.BlockSpec((tm, kdim), lambda j, i: (i, 0))
    col = lambda kdim: pl.BlockSpec((kdim, tn), lambda j, i: (0, j))
    return pl.pallas_call(
        _merge_kernel,
        grid=(dn // tn, n // tm),
        in_specs=[row(d), row(ya.shape[1]), row(yb.shape[1]), col(d), col(d), col(ya.shape[1]), col(yb.shape[1])],
        out_specs=pl.BlockSpec((tm, tn), lambda j, i: (i, j)),
        out_shape=jax.ShapeDtypeStruct((n, dn), BF16),
        compiler_params=_params(("parallel", "parallel")),
        name="merge",
    )(u, ya, yb, wga, wgb, wua, wub)


def _router_kernel(x_ref, g_ref, wr_ref, v_ref, aff_ref):
    x = x_ref[...]
    ms = jnp.mean(x * x, axis=-1, keepdims=True)
    v = x * lax.rsqrt(ms + NORM_EPS) * g_ref[...]
    v_ref[...] = v
    vh, vl = _split2(v)
    wh, wl = _split2(wr_ref[...])
    logits = (jnp.dot(vh, wh, preferred_element_type=F32) + jnp.dot(vl, wh, preferred_element_type=F32)
              + jnp.dot(vh, wl, preferred_element_type=F32))
    m = jnp.max(logits, axis=-1, keepdims=True)
    e = jnp.exp(logits - m)
    aff_ref[...] = e / jnp.sum(e, axis=-1, keepdims=True)


def _router(x, g, w_router, tm=512):
    n, d = x.shape
    ne = w_router.shape[1]
    return pl.pallas_call(
        _router_kernel,
        grid=(n // tm,),
        in_specs=[pl.BlockSpec((tm, d), lambda i: (i, 0)), pl.BlockSpec((1, d), lambda i: (0, 0)),
                  pl.BlockSpec((d, ne), lambda i: (0, 0))],
        out_specs=[pl.BlockSpec((tm, d), lambda i: (i, 0)), pl.BlockSpec((tm, ne), lambda i: (i, 0))],
        out_shape=[jax.ShapeDtypeStruct((n, d), F32), jax.ShapeDtypeStruct((n, ne), F32)],
        compiler_params=_params(("parallel",)),
        name="router",
    )(x, g.reshape(1, d), w_router)


def _moe_kernel(idx_ref, gates_ref, wg_ref, wu_ref, wd_ref, vn_hbm, x_hbm, xo_hbm,
                rows, xs, acc, sem_in, sem_out):
    del x_hbm
    e = pl.program_id(0)
    f = pl.program_id(1)
    nf = pl.num_programs(1)
    n_rows = rows.shape[0]

    def gather_rows(src_hbm):
        def start(i, c):
            pltpu.make_async_copy(src_hbm.at[pl.ds(idx_ref[e * n_rows + i], 1)], rows.at[pl.ds(i, 1)],
                                  sem_in).start()
            return c
        lax.fori_loop(0, n_rows, start, 0, unroll=DMA_UNROLL)

    def wait_rows(src_hbm):
        pltpu.make_async_copy(src_hbm.at[pl.ds(0, n_rows)], rows, sem_in).wait()

    @pl.when(f == 0)
    def _():
        gather_rows(vn_hbm)
        wait_rows(vn_hbm)
        xs[...] = rows[...].astype(BF16)
        gather_rows(xo_hbm)
        acc[...] = jnp.zeros_like(acc)

    x = xs[...]
    hg = jnp.dot(x, wg_ref[0, 0].astype(BF16), preferred_element_type=F32)
    hu = jnp.dot(x, wu_ref[0, 0].astype(BF16), preferred_element_type=F32)
    h = (hg * _sigmoid(hg) * hu).astype(BF16)
    acc[...] += jnp.dot(h, wd_ref[0, 0].astype(BF16), preferred_element_type=F32)

    @pl.when(f == nf - 1)
    def _():
        wait_rows(xo_hbm)
        rows[...] = rows[...] + acc[...] * gates_ref[0]

        def start(i, c):
            pltpu.make_async_copy(rows.at[pl.ds(i, 1)], xo_hbm.at[pl.ds(idx_ref[e * n_rows + i], 1)],
                                  sem_out).start()
            return c
        lax.fori_loop(0, n_rows, start, 0, unroll=DMA_UNROLL)
        pltpu.make_async_copy(rows, xo_hbm.at[pl.ds(0, n_rows)], sem_out).wait()


def _moe(x, vn, idx, gates, w_gate, w_up, w_down, layer, tf=256):
    n, d = x.shape
    ne, n_rows = idx.shape
    ff = w_gate.shape[3]
    grid_spec = pltpu.PrefetchScalarGridSpec(
        num_scalar_prefetch=1,
        grid=(ne, ff // tf),
        in_specs=[pl.BlockSpec((1, n_rows, 1), lambda e, f, idx: (e, 0, 0)),
                  pl.BlockSpec((1, 1, d, tf), lambda e, f, idx: (layer, e, 0, f)),
                  pl.BlockSpec((1, 1, d, tf), lambda e, f, idx: (layer, e, 0, f)),
                  pl.BlockSpec((1, 1, tf, d), lambda e, f, idx: (layer, e, f, 0)),
                  pl.BlockSpec(memory_space=pl.ANY),
                  pl.BlockSpec(memory_space=pl.ANY)],
        out_specs=pl.BlockSpec(memory_space=pl.ANY),
        scratch_shapes=[pltpu.VMEM((n_rows, d), F32), pltpu.VMEM((n_rows, d), BF16),
                        pltpu.VMEM((n_rows, d), F32), pltpu.SemaphoreType.DMA, pltpu.SemaphoreType.DMA],
    )
    return pl.pallas_call(
        _moe_kernel,
        grid_spec=grid_spec,
        out_shape=jax.ShapeDtypeStruct((n, d), F32),
        input_output_aliases={6: 0},
        compiler_params=_params(("arbitrary", "arbitrary")),
        name="moe",
    )(idx.reshape(ne * n_rows), gates, w_gate, w_up, w_down, vn, x)


def _pad_rows(w, rows):
    return jnp.pad(w, [(0, 0)] * (w.ndim - 2) + [(0, rows - w.shape[-2]), (0, 0)])


def _pad_cols(w, cols):
    return jnp.pad(w, [(0, 0)] * (w.ndim - 1) + [(0, cols - w.shape[-1])])


def kernel(x, norm_mix_g, w_in, mu_shift, w0, w_lora_up, a0, a_lora_up, g_lora_up, k_k, k_a, r_k,
           lnx_g, lnx_b, pool_w, pool_scale, w_up_a, w_up_b, w_o, norm_moe_g, w_router,
           w_gate_e, w_up_e, w_down_e, final_g):
    batch, seq, d = x.shape
    n = batch * seq
    depth = w_in.shape[0]
    d_pool = pool_w.shape[1] * pool_w.shape[2]
    lora = w_lora_up.shape[2]
    gate_lora = g_lora_up.shape[1]
    c_lora = 3 * D_RWKV
    c_pool = c_lora + 4 * lora + gate_lora
    c_ga = c_pool + d_pool
    c_gb = c_ga + d
    cap = CAPACITY_FACTOR * seq // N_EXPERTS

    def lora_cols(w):
        segs = [_pad_cols(w[..., c_lora + j * lora:c_lora + (j + 1) * lora], LORA_PAD) for j in range(4)]
        segs.append(_pad_cols(w[..., c_lora + 4 * lora:c_pool], GATE_PAD))
        return jnp.concatenate(segs, axis=-1)

    xf = x.reshape(n, d)
    for l in range(depth):
        wl = w_in[l]
        u = _rmsnorm(xf, norm_mix_g[l], BF16)
        p_rkv = _matmul(u, wl[:, :c_lora].astype(BF16), tn=1024, name="proj_rkv")
        p_lora = _matmul(u, lora_cols(wl).astype(BF16), tn=4 * LORA_PAD + GATE_PAD, name="proj_lora")
        p_pool = _matmul(u, wl[:, c_pool:c_ga].astype(BF16), tn=1024, name="proj_pool")
        mu = mu_shift[l]
        prep = _rwkv_prep(
            p_rkv, p_lora, mu[:c_lora].reshape(1, -1), lora_cols(mu).reshape(1, -1),
            w0[l], _pad_rows(w_lora_up[l], LORA_PAD).astype(BF16),
            a0[l], _pad_rows(a_lora_up[l], LORA_PAD).astype(BF16),
            _pad_rows(g_lora_up[l], GATE_PAD).astype(BF16),
            k_k[l].reshape(1, -1), k_a[l].reshape(1, -1), r_k[l].reshape(1, -1), seq)
        r, kk, v, lw_f, lw_b, g_f, g_b, k_f, k_b, b_f, b_b, g, bonus = prep
        yf, yb = _wkv(r, kk, v, lw_f, lw_b, g_f, g_b, k_f, k_b, b_f, b_b, batch)
        y_a = _rwkv_post(yf, yb, g, bonus, lnx_g[l].reshape(1, -1), lnx_b[l].reshape(1, -1))
        y_b = _pool(p_pool, pool_w[l].astype(BF16), pool_scale[l].reshape(1, -1), seq)
        merged = _merge(u, y_a, y_b, wl[:, c_ga:c_gb].astype(BF16), wl[:, c_gb:].astype(BF16),
                        w_up_a[l].astype(BF16), w_up_b[l].astype(BF16))
        xf = _matmul(merged, w_o[l].astype(BF16), residual=xf, name="out_proj")

        vn, aff = _router(xf, norm_moe_g[l], w_router[l])
        aff_t = jnp.swapaxes(aff.reshape(batch, seq, N_EXPERTS), 1, 2)
        gates, idx = lax.top_k(aff_t, cap)
        rows = idx + (jnp.arange(batch, dtype=idx.dtype) * seq)[:, None, None]
        rows = jnp.swapaxes(rows, 0, 1).reshape(N_EXPERTS, batch * cap).astype(jnp.int32)
        gates = jnp.swapaxes(gates, 0, 1).reshape(N_EXPERTS, batch * cap, 1)
        xf = _moe(xf, vn, rows, gates, w_gate_e, w_up_e, w_down_e, l)
    out = _rmsnorm(xf, final_g, F32)
    return out.reshape(batch, seq, d)
```

```python
import functools

import numpy as np
import jax
import jax.numpy as jnp
from jax import lax
from jax.experimental import pallas as pl
from jax.experimental.pallas import tpu as pltpu

F32 = jnp.float32
BF16 = jnp.bfloat16

HEAD_DIM = 64
N_HEADS = 16
D_RWKV = N_HEADS * HEAD_DIM
LORA_PAD = 128
GATE_PAD = 256
POOL_WINDOWS = (2, 4, 8, 16)
POOL_GROUP = 256
N_EXPERTS = 16
CAPACITY_FACTOR = 2
NORM_EPS = 1e-6
LNX_EPS = 64e-5

CHUNK = 64
GROUP_LANES = 256
HEADS_PER_GROUP = GROUP_LANES // HEAD_DIM
HALO = 8
PREP_ROWS = 256
DMA_UNROLL = 8
VMEM_LIMIT = 56 * 1024 * 1024


def _params(sem):
    return pltpu.CompilerParams(dimension_semantics=sem, vmem_limit_bytes=VMEM_LIMIT)


def _dot(a, b):
    return jnp.dot(a.astype(BF16), b.astype(BF16), preferred_element_type=F32)


def _split2(x):
    hi = x.astype(BF16)
    lo = (x - hi.astype(F32)).astype(BF16)
    return hi, lo


def _split3(x):
    hi = x.astype(BF16)
    r1 = x - hi.astype(F32)
    mid = r1.astype(BF16)
    lo = (r1 - mid.astype(F32)).astype(BF16)
    return hi, mid, lo


def _head_ones(n):
    r = lax.broadcasted_iota(jnp.int32, (n, n), 0) // HEAD_DIM
    c = lax.broadcasted_iota(jnp.int32, (n, n), 1) // HEAD_DIM
    return r == c


def _head_sum(x, ones_bf16):
    hi, lo = _split2(x)
    return (jnp.dot(hi, ones_bf16, preferred_element_type=F32)
            + jnp.dot(lo, ones_bf16, preferred_element_type=F32))


def _rmsnorm_kernel(x_ref, g_ref, o_ref):
    x = x_ref[...]
    ms = jnp.mean(x * x, axis=-1, keepdims=True)
    o_ref[...] = (x * lax.rsqrt(ms + NORM_EPS) * g_ref[...]).astype(o_ref.dtype)


def _rmsnorm(x, g, out_dtype, tm=512):
    n, d = x.shape
    return pl.pallas_call(
        _rmsnorm_kernel,
        grid=(n // tm,),
        in_specs=[pl.BlockSpec((tm, d), lambda i: (i, 0)), pl.BlockSpec((1, d), lambda i: (0, 0))],
        out_specs=pl.BlockSpec((tm, d), lambda i: (i, 0)),
        out_shape=jax.ShapeDtypeStruct((n, d), out_dtype),
        compiler_params=_params(("parallel",)),
        name="rmsnorm",
    )(x, g.reshape(1, d))


def _mm_kernel(a_ref, w_ref, o_ref):
    o_ref[...] = jnp.dot(a_ref[...], w_ref[...], preferred_element_type=F32).astype(o_ref.dtype)


def _mm_res_kernel(a_ref, w_ref, r_ref, o_ref):
    o_ref[...] = r_ref[...] + jnp.dot(a_ref[...], w_ref[...], preferred_element_type=F32)


def _matmul(a, w, out_dtype=F32, residual=None, tm=512, tn=512, name="matmul"):
    m, k = a.shape
    nn = w.shape[1]
    tn = min(tn, nn)
    in_specs = [pl.BlockSpec((tm, k), lambda j, i: (i, 0)), pl.BlockSpec((k, tn), lambda j, i: (0, j))]
    args = [a, w]
    kern = _mm_kernel
    if residual is not None:
        in_specs.append(pl.BlockSpec((tm, tn), lambda j, i: (i, j)))
        args.append(residual)
        kern = _mm_res_kernel
    return pl.pallas_call(
        kern,
        grid=(nn // tn, m // tm),
        in_specs=in_specs,
        out_specs=pl.BlockSpec((tm, tn), lambda j, i: (i, j)),
        out_shape=jax.ShapeDtypeStruct((m, nn), out_dtype),
        compiler_params=_params(("parallel", "parallel")),
        name=name,
    )(*args)


def _token_shift(p_ref, prev_ref, next_ref, mu, first, last):
    p = p_ref[...]
    tm = p.shape[0]
    prev_row = jnp.where(first, 0.0, prev_ref[HALO - 1:HALO, :])
    next_row = jnp.where(last, 0.0, next_ref[0:1, :])
    rows = lax.broadcasted_iota(jnp.int32, p.shape, 0)
    p_prev = jnp.where(rows == 0, prev_row, pltpu.roll(p, 1, 0))
    p_next = jnp.where(rows == tm - 1, next_row, pltpu.roll(p, tm - 1, 0))
    return p + mu * (0.5 * (p_prev + p_next) - p)


def _softplus(x):
    return jnp.maximum(x, 0.0) + jnp.log(1.0 + jnp.exp(-jnp.abs(x)))


def _sigmoid(x):
    return 1.0 / (1.0 + jnp.exp(-x))


def _prep_kernel(tiles_per_seq,
                 r_ref, rp_ref, rn_ref, k_ref, kp_ref, kn_ref, v_ref, vp_ref, vn_ref,
                 lo_ref, lop_ref, lon_ref,
                 mur_ref, muk_ref, muv_ref, mulo_ref,
                 w0_ref, wup_ref, a0_ref, aup_ref, gup_ref, kk_ref, ka_ref, rk_ref, tri_ref,
                 o_r, o_kk, o_v, o_lwf, o_lwb, o_gf, o_gb, o_kf, o_kb, o_bf, o_bb, o_g, o_bonus):
    i = pl.program_id(0)
    first = (i % tiles_per_seq) == 0
    last = (i % tiles_per_seq) == tiles_per_seq - 1
    r = _token_shift(r_ref, rp_ref, rn_ref, mur_ref[...], first, last)
    k = _token_shift(k_ref, kp_ref, kn_ref, muk_ref[...], first, last)
    v = _token_shift(v_ref, vp_ref, vn_ref, muv_ref[...], first, last)
    lo = _token_shift(lo_ref, lop_ref, lon_ref, mulo_ref[...], first, last)
    ones = _head_ones(GROUP_LANES).astype(BF16)

    kk0 = k * kk_ref[...]
    nrm = jnp.sqrt(_head_sum(kk0 * kk0, ones))
    kk = kk0 / jnp.maximum(nrm, 1e-12)
    o_r[...] = r
    o_kk[...] = kk
    o_v[...] = v

    k_dirs = []
    for d, (o_lw, o_gc, o_k, o_b) in enumerate(((o_lwf, o_gf, o_kf, o_bf), (o_lwb, o_gb, o_kb, o_bb))):
        wd = lo[:, d * LORA_PAD:(d + 1) * LORA_PAD]
        ad = lo[:, (2 + d) * LORA_PAD:(3 + d) * LORA_PAD]
        w_pre = w0_ref[d:d + 1, :] + _dot(jnp.tanh(wd), wup_ref[d])
        lw = -jnp.exp(-_softplus(-w_pre) - 0.5)
        o_lw[...] = lw
        o_gc[...] = sum(jnp.dot(tri_ref[d], part, preferred_element_type=F32) for part in _split3(lw))
        a = _sigmoid(a0_ref[d:d + 1, :] + _dot(ad, aup_ref[d]))
        k_d = k * (1.0 + (a - 1.0) * ka_ref[...])
        o_k[...] = k_d
        o_b[...] = kk * a
        k_dirs.append(k_d)
    gd = lo[:, 4 * LORA_PAD:4 * LORA_PAD + GATE_PAD]
    o_g[...] = _dot(_sigmoid(gd), gup_ref[...])
    k_bonus = 0.5 * (k_dirs[0] + k_dirs[1])
    o_bonus[...] = _head_sum(r * k_bonus * rk_ref[...], ones) * v


def _chunk_cumsum_matrices(tm):
    t = np.arange(tm)[:, None]
    s = np.arange(tm)[None, :]
    same_chunk = (t // CHUNK) == (s // CHUNK)
    return jnp.asarray(np.stack([same_chunk & (s <= t), same_chunk & (s >= t)]), BF16)


def _rwkv_prep(p_rkv, p_lora, mu_rkv, mu_lora, w0, wup, a0, aup, gup, k_k, k_a, r_k, seq_len, tm=PREP_ROWS):
    n = p_rkv.shape[0]
    ng = D_RWKV // GROUP_LANES
    nlo = p_lora.shape[1]
    tph = tm // HALO
    nh = n // HALO

    def main(col0):
        return pl.BlockSpec((tm, GROUP_LANES), lambda i, g: (i, col0 + g))

    def prev(col0):
        return pl.BlockSpec((HALO, GROUP_LANES), lambda i, g: (jnp.maximum(i * tph - 1, 0), col0 + g))

    def nxt(col0):
        return pl.BlockSpec((HALO, GROUP_LANES), lambda i, g: (jnp.minimum((i + 1) * tph, nh - 1), col0 + g))

    in_specs = []
    args = []
    for c in range(3):
        in_specs += [main(c * ng), prev(c * ng), nxt(c * ng)]
        args += [p_rkv, p_rkv, p_rkv]
    in_specs += [pl.BlockSpec((tm, nlo), lambda i, g: (i, 0)),
                 pl.BlockSpec((HALO, nlo), lambda i, g: (jnp.maximum(i * tph - 1, 0), 0)),
                 pl.BlockSpec((HALO, nlo), lambda i, g: (jnp.minimum((i + 1) * tph, nh - 1), 0))]
    args += [p_lora, p_lora, p_lora]
    for c in range(3):
        in_specs.append(pl.BlockSpec((1, GROUP_LANES), lambda i, g, c=c: (0, c * ng + g)))
        args.append(mu_rkv)
    in_specs.append(pl.BlockSpec((1, nlo), lambda i, g: (0, 0)))
    args.append(mu_lora)
    vec2 = pl.BlockSpec((2, GROUP_LANES), lambda i, g: (0, g))
    lora2 = pl.BlockSpec((2, LORA_PAD, GROUP_LANES), lambda i, g: (0, 0, g))
    vec1 = pl.BlockSpec((1, GROUP_LANES), lambda i, g: (0, g))
    in_specs += [vec2, lora2, vec2, lora2, pl.BlockSpec((GATE_PAD, GROUP_LANES), lambda i, g: (0, g)),
                 vec1, vec1, vec1, pl.BlockSpec((2, tm, tm), lambda i, g: (0, 0, 0))]
    args += [w0, wup, a0, aup, gup, k_k, k_a, r_k, _chunk_cumsum_matrices(tm)]
    out_spec = pl.BlockSpec((tm, GROUP_LANES), lambda i, g: (i, g))
    out_shape = jax.ShapeDtypeStruct((n, D_RWKV), F32)
    return pl.pallas_call(
        functools.partial(_prep_kernel, seq_len // tm),
        grid=(n // tm, ng),
        in_specs=in_specs,
        out_specs=[out_spec] * 13,
        out_shape=[out_shape] * 13,
        compiler_params=_params(("parallel", "parallel")),
        name="rwkv_prep",
    )(*args)


M_BEFORE_F, M_INCL_F, M_BEFORE_B, M_INCL_B, M_EYE, M_BLK8, M_OFF = 0, 1, 2, 3, 4, 5, 6


def _wkv_consts():
    c = CHUNK
    row = np.arange(c)[:, None]
    col = (np.arange(GROUP_LANES) % c)[None, :]
    same = col == row
    masks = [col < row, (col < row) | same, col > row, (col > row) | same, same, (row // 8) == (col // 8)]
    s = 8
    while s < c:
        masks.append(((row // (2 * s)) == (col // (2 * s))) & ((row // s) != (col // s)))
        s *= 2
    head = np.arange(GROUP_LANES) // HEAD_DIM
    blk = head[:, None] == head[None, :]
    return jnp.asarray(np.stack(masks), F32), jnp.asarray(blk, BF16)


def _wkv_chunks(streams, m_ref, blk_ref):
    c = CHUNK
    ns = range(len(streams))
    r, kk, v, lw, g, k, b, s_prev, rev = (list(col) for col in zip(*streams))
    before = [m_ref[M_BEFORE_B if rev[i] else M_BEFORE_F] for i in ns]
    incl = [m_ref[M_INCL_B if rev[i] else M_INCL_F] for i in ns]
    eye = m_ref[M_EYE]

    def stack(x):
        return jnp.concatenate([x.astype(BF16)] * HEADS_PER_GROUP, axis=0) * blk_ref[...]

    def dot(a, bm):
        return jnp.dot(a.astype(BF16), bm, preferred_element_type=F32)

    def dot_nt(a, bm):
        return lax.dot_general(a, bm, (((1,), (1,)), ((), ())), preferred_element_type=F32)

    g_tot = [g[i][0:1, :] if rev[i] else g[i][c - 1:c, :] for i in ns]
    e_inv = [jnp.exp(-g[i]) for i in ns]
    lhs = [jnp.concatenate([kk[i] * jnp.exp(g[i] - lw[i]), r[i] * jnp.exp(g[i])], axis=0).astype(BF16)
           for i in ns]
    a_k = [dot_nt(lhs[i], stack(k[i] * e_inv[i])) for i in ns]
    a_b = [dot_nt(lhs[i], stack(b[i] * e_inv[i])) for i in ns]
    p0 = [dot_nt(lhs[i], s_prev[i].astype(BF16)) for i in ns]
    n_mat = [a_b[i][:c] * before[i] for i in ns]
    av = [dot(jnp.concatenate([a_k[i][:c] * before[i], a_k[i][c:] * incl[i]], axis=0), stack(v[i])) for i in ns]

    n0 = [n_mat[i] * m_ref[M_BLK8] for i in ns]
    n2 = [dot(n0[i], stack(n0[i])) for i in ns]
    n4 = [dot(n2[i], stack(n2[i])) for i in ns]
    t = [dot(eye - n0[i], stack(eye + n2[i])) for i in ns]
    t = [dot(t[i], stack(eye + n4[i])) for i in ns]
    for lvl in range(M_OFF, m_ref.shape[0]):
        x = [dot(t[i], stack(n_mat[i] * m_ref[lvl])) for i in ns]
        t = [t[i] - dot(x[i], stack(t[i])) for i in ns]
    u = [dot(t[i], stack(p0[i][:c] + av[i][:c])) for i in ns]
    y = [p0[i][c:] + av[i][c:] - dot(a_b[i][c:] * incl[i], stack(u[i])) for i in ns]
    out = []
    for i in ns:
        e_tail = jnp.exp(g_tot[i] - g[i])
        vu = jnp.concatenate([v[i], -u[i]], axis=0).astype(BF16)
        kb = jnp.concatenate([k[i] * e_tail, b[i] * e_tail], axis=0).astype(BF16)
        ds = lax.dot_general(vu, kb, (((0,), (0,)), ((), ())), preferred_element_type=F32)
        out.append((y[i], s_prev[i] * jnp.exp(g_tot[i]) + ds * blk_ref[...].astype(F32)))
    return out


def _wkv_kernel(n_chunks, m_ref, blk_ref,
                rf, kkf, vf, lwf, gf, kf, bf, rb, kkb, vb, lwb, gb, kb, bb,
                yf_ref, yb_ref, sf_ref, sb_ref):
    @pl.when(pl.program_id(1) == 0)
    def _():
        sf_ref[...] = jnp.zeros_like(sf_ref)
        sb_ref[...] = jnp.zeros_like(sb_ref)

    for ci in range(n_chunks):
        fs = slice(ci * CHUNK, (ci + 1) * CHUNK)
        bs = slice((n_chunks - 1 - ci) * CHUNK, (n_chunks - ci) * CHUNK)
        streams = []
        for gi in range(sf_ref.shape[0]):
            ls = slice(gi * GROUP_LANES, (gi + 1) * GROUP_LANES)
            streams.append((rf[fs, ls], kkf[fs, ls], vf[fs, ls], lwf[fs, ls], gf[fs, ls], kf[fs, ls],
                            bf[fs, ls], sf_ref[gi], False))
            streams.append((rb[bs, ls], kkb[bs, ls], vb[bs, ls], lwb[bs, ls], gb[bs, ls], kb[bs, ls],
                            bb[bs, ls], sb_ref[gi], True))
        results = _wkv_chunks(streams, m_ref, blk_ref)
        for gi in range(sf_ref.shape[0]):
            ls = slice(gi * GROUP_LANES, (gi + 1) * GROUP_LANES)
            (y_f, s_f), (y_b, s_b) = results[2 * gi], results[2 * gi + 1]
            yf_ref[fs, ls] = y_f
            sf_ref[gi] = s_f
            yb_ref[bs, ls] = y_b
            sb_ref[gi] = s_b


def _wkv(r, kk, v, lw_f, lw_b, g_f, g_b, k_f, k_b, b_f, b_b, batch, rows_per_step=CHUNK):
    n = r.shape[0]
    seq = n // batch
    ns = seq // rows_per_step
    ng = D_RWKV // GROUP_LANES
    masks, blk = _wkv_consts()
    fwd = pl.BlockSpec((rows_per_step, D_RWKV), lambda bi, s: (bi * ns + s, 0))
    bwd = pl.BlockSpec((rows_per_step, D_RWKV), lambda bi, s: (bi * ns + ns - 1 - s, 0))
    out_shape = jax.ShapeDtypeStruct((n, D_RWKV), F32)
    state = pltpu.VMEM((ng, GROUP_LANES, GROUP_LANES), F32)
    return pl.pallas_call(
        functools.partial(_wkv_kernel, rows_per_step // CHUNK),
        grid=(batch, ns),
        in_specs=[pl.BlockSpec(masks.shape, lambda bi, s: (0, 0, 0)), pl.BlockSpec(blk.shape, lambda bi, s: (0, 0))]
        + [fwd] * 7 + [bwd] * 7,
        out_specs=[fwd, bwd],
        out_shape=[out_shape, out_shape],
        scratch_shapes=[state, state],
        compiler_params=_params(("parallel", "arbitrary")),
        name="wkv7",
    )(masks, blk, r, kk, v, lw_f, g_f, k_f, b_f, r, kk, v, lw_b, g_b, k_b, b_b)


def _post_kernel(yf_ref, yb_ref, g_ref, bonus_ref, lg_ref, lb_ref, o_ref):
    ones = _head_ones(GROUP_LANES).astype(BF16)
    y = yf_ref[...] + yb_ref[...]
    mu = _head_sum(y, ones) * (1.0 / HEAD_DIM)
    yc = y - mu
    var = _head_sum(yc * yc, ones) * (1.0 / HEAD_DIM)
    yn = yc * lax.rsqrt(var + LNX_EPS) * lg_ref[...] + lb_ref[...]
    o_ref[...] = ((yn + bonus_ref[...]) * g_ref[...]).astype(o_ref.dtype)


def _rwkv_post(yf, yb, g, bonus, lnx_g, lnx_b, tm=512):
    n = yf.shape[0]
    ng = D_RWKV // GROUP_LANES
    big = pl.BlockSpec((tm, GROUP_LANES), lambda i, j: (i, j))
    vec = pl.BlockSpec((1, GROUP_LANES), lambda i, j: (0, j))
    return pl.pallas_call(
        _post_kernel,
        grid=(n // tm, ng),
        in_specs=[big] * 4 + [vec] * 2,
        out_specs=big,
        out_shape=jax.ShapeDtypeStruct((n, D_RWKV), BF16),
        compiler_params=_params(("parallel", "parallel")),
        name="rwkv_post",
    )(yf, yb, g, bonus, lnx_g, lnx_b)


POOL_PAD_ROWS = 128


def _pool_kernel(tiles_per_seq, p_ref, prev_ref, next_ref, w_ref, scale_ref, o_ref):
    i = pl.program_id(0)
    gi = pl.program_id(1)
    p = p_ref[...]
    tm = p.shape[0]
    ext = jnp.concatenate(
        [prev_ref[...], p, next_ref[...], jnp.zeros((POOL_PAD_ROWS - 2 * HALO, p.shape[1]), F32)], axis=0)
    ke = tm + POOL_PAD_ROWS
    half = jnp.left_shift(1, gi)
    t = lax.broadcasted_iota(jnp.int32, (tm, ke), 0)
    s = lax.broadcasted_iota(jnp.int32, (tm, ke), 1) - HALO
    pos = (i % tiles_per_seq) * tm + s
    seq = tiles_per_seq * tm
    band = (s >= t - half) & (s < t + half) & (pos >= 0) & (pos < seq) & (s < tm + HALO)
    band = jnp.where(band, 1.0, 0.0).astype(BF16)
    hi, lo = _split2(ext)
    tot = jnp.dot(band, hi, preferred_element_type=F32) + jnp.dot(band, lo, preferred_element_type=F32)
    tpos = (i % tiles_per_seq) * tm + lax.broadcasted_iota(jnp.int32, (tm, 1), 0)
    cnt = jnp.minimum(tpos + half, seq) - jnp.maximum(tpos - half, 0)
    z = tot / cnt.astype(F32) - p
    o_ref[...] = (_dot(z, w_ref[0]) * scale_ref[...]).astype(o_ref.dtype)


def _pool(p_pool, pool_w, pool_scale, seq_len, tm=256):
    n = p_pool.shape[0]
    ng = len(POOL_WINDOWS)
    tph = tm // HALO
    nh = n // HALO
    return pl.pallas_call(
        functools.partial(_pool_kernel, seq_len // tm),
        grid=(n // tm, ng),
        in_specs=[pl.BlockSpec((tm, POOL_GROUP), lambda i, g: (i, g)),
                  pl.BlockSpec((HALO, POOL_GROUP), lambda i, g: (jnp.maximum(i * tph - 1, 0), g)),
                  pl.BlockSpec((HALO, POOL_GROUP), lambda i, g: (jnp.minimum((i + 1) * tph, nh - 1), g)),
                  pl.BlockSpec((1, POOL_GROUP, POOL_GROUP), lambda i, g: (g, 0, 0)),
                  pl.BlockSpec((1, POOL_GROUP), lambda i, g: (0, g))],
        out_specs=pl.BlockSpec((tm, POOL_GROUP), lambda i, g: (i, g)),
        out_shape=jax.ShapeDtypeStruct((n, ng * POOL_GROUP), BF16),
        compiler_params=_params(("parallel", "parallel")),
        name="pool",
    )(p_pool, p_pool, p_pool, pool_w, pool_scale)


def _merge_kernel(u_ref, ya_ref, yb_ref, wga_ref, wgb_ref, wua_ref, wub_ref, o_ref):
    u = u_ref[...]
    ga = jnp.dot(u, wga_ref[...], preferred_element_type=F32)
    gb = jnp.dot(u, wgb_ref[...], preferred_element_type=F32)
    ua = jnp.dot(ya_ref[...], wua_ref[...], preferred_element_type=F32)
    ub = jnp.dot(yb_ref[...], wub_ref[...], preferred_element_type=F32)
    o_ref[...] = (_sigmoid(ga) * ua + _sigmoid(gb) * ub).astype(o_ref.dtype)


def _merge(u, ya, yb, wga, wgb, wua, wub, tm=512, tn=512):
    n, d = u.shape
    dn = wga.shape[1]
    row = lambda kdim: pl.BlockSpec((tm, kdim), lambda j, i: (i, 0))
    col = lambda kdim: pl.BlockSpec((kdim, tn), lambda j, i: (0, j))
    return pl.pallas_call(
        _merge_kernel,
        grid=(dn // tn, n // tm),
        in_specs=[row(d), row(ya.shape[1]), row(yb.shape[1]), col(d), col(d), col(ya.shape[1]), col(yb.shape[1])],
        out_specs=pl.BlockSpec((tm, tn), lambda j, i: (i, j)),
        out_shape=jax.ShapeDtypeStruct((n, dn), BF16),
        compiler_params=_params(("parallel", "parallel")),
        name="merge",
    )(u, ya, yb, wga, wgb, wua, wub)


def _router_kernel(x_ref, g_ref, wr_ref, v_ref, aff_ref):
    x = x_ref[...]
    ms = jnp.mean(x * x, axis=-1, keepdims=True)
    v = x * lax.rsqrt(ms + NORM_EPS) * g_ref[...]
    v_ref[...] = v
    vh, vl = _split2(v)
    wh, wl = _split2(wr_ref[...])
    logits = (jnp.dot(vh, wh, preferred_element_type=F32) + jnp.dot(vl, wh, preferred_element_type=F32)
              + jnp.dot(vh, wl, preferred_element_type=F32))
    m = jnp.max(logits, axis=-1, keepdims=True)
    e = jnp.exp(logits - m)
    aff_ref[...] = e / jnp.sum(e, axis=-1, keepdims=True)


def _router(x, g, w_router, tm=512):
    n, d = x.shape
    ne = w_router.shape[1]
    return pl.pallas_call(
        _router_kernel,
        grid=(n // tm,),
        in_specs=[pl.BlockSpec((tm, d), lambda i: (i, 0)), pl.BlockSpec((1, d), lambda i: (0, 0)),
                  pl.BlockSpec((d, ne), lambda i: (0, 0))],
        out_specs=[pl.BlockSpec((tm, d), lambda i: (i, 0)), pl.BlockSpec((tm, ne), lambda i: (i, 0))],
        out_shape=[jax.ShapeDtypeStruct((n, d), F32), jax.ShapeDtypeStruct((n, ne), F32)],
        compiler_params=_params(("parallel",)),
        name="router",
    )(x, g.reshape(1, d), w_router)


def _moe_kernel(idx_ref, gates_ref, wg_ref, wu_ref, wd_ref, vn_hbm, x_hbm, xo_hbm,
                rows, xs, acc, sem_in, sem_out):
    del x_hbm
    e = pl.program_id(0)
    f = pl.program_id(1)
    nf = pl.num_programs(1)
    n_rows = rows.shape[0]

    def gather_rows(src_hbm):
        def start(blk_i, c):
            for j in range(DMA_UNROLL):
                i = blk_i * DMA_UNROLL + j
                pltpu.make_async_copy(src_hbm.at[pl.ds(idx_ref[e * n_rows + i], 1)], rows.at[pl.ds(i, 1)],
                                      sem_in).start(priority=j % 2)
            return c
        lax.fori_loop(0, n_rows // DMA_UNROLL, start, 0)

    def wait_rows(src_hbm):
        pltpu.make_async_copy(src_hbm.at[pl.ds(0, n_rows)], rows, sem_in).wait()

    @pl.when(f == 0)
    def _():
        gather_rows(vn_hbm)
        wait_rows(vn_hbm)
        xs[...] = rows[...].astype(BF16)
        gather_rows(xo_hbm)
        acc[...] = jnp.zeros_like(acc)

    x = xs[...]
    hg = jnp.dot(x, wg_ref[0, 0].astype(BF16), preferred_element_type=F32)
    hu = jnp.dot(x, wu_ref[0, 0].astype(BF16), preferred_element_type=F32)
    h = (hg * _sigmoid(hg) * hu).astype(BF16)
    acc[...] += jnp.dot(h, wd_ref[0, 0].astype(BF16), preferred_element_type=F32)

    @pl.when(f == nf - 1)
    def _():
        wait_rows(xo_hbm)
        rows[...] = rows[...] + acc[...] * gates_ref[0]

        def start(blk_i, c):
            for j in range(DMA_UNROLL):
                i = blk_i * DMA_UNROLL + j
                pltpu.make_async_copy(rows.at[pl.ds(i, 1)], xo_hbm.at[pl.ds(idx_ref[e * n_rows + i], 1)],
                                      sem_out).start(priority=j % 2)
            return c
        lax.fori_loop(0, n_rows // DMA_UNROLL, start, 0)
        pltpu.make_async_copy(rows, xo_hbm.at[pl.ds(0, n_rows)], sem_out).wait()


def _moe(x, vn, idx, gates, w_gate, w_up, w_down, layer, tf=256):
    n, d = x.shape
    ne, n_rows = idx.shape
    ff = w_gate.shape[3]
    grid_spec = pltpu.PrefetchScalarGridSpec(
        num_scalar_prefetch=1,
        grid=(ne, ff // tf),
        in_specs=[pl.BlockSpec((1, n_rows, 1), lambda e, f, idx: (e, 0, 0)),
                  pl.BlockSpec((1, 1, d, tf), lambda e, f, idx: (layer, e, 0, f)),
                  pl.BlockSpec((1, 1, d, tf), lambda e, f, idx: (layer, e, 0, f)),
                  pl.BlockSpec((1, 1, tf, d), lambda e, f, idx: (layer, e, f, 0)),
                  pl.BlockSpec(memory_space=pl.ANY),
                  pl.BlockSpec(memory_space=pl.ANY)],
        out_specs=pl.BlockSpec(memory_space=pl.ANY),
        scratch_shapes=[pltpu.VMEM((n_rows, d), F32), pltpu.VMEM((n_rows, d), BF16),
                        pltpu.VMEM((n_rows, d), F32), pltpu.SemaphoreType.DMA, pltpu.SemaphoreType.DMA],
    )
    return pl.pallas_call(
        _moe_kernel,
        grid_spec=grid_spec,
        out_shape=jax.ShapeDtypeStruct((n, d), F32),
        input_output_aliases={6: 0},
        compiler_params=_params(("arbitrary", "arbitrary")),
        name="moe",
    )(idx.reshape(ne * n_rows), gates, w_gate, w_up, w_down, vn, x)


def _pad_rows(w, rows):
    return jnp.pad(w, [(0, 0)] * (w.ndim - 2) + [(0, rows - w.shape[-2]), (0, 0)])


def _pad_cols(w, cols):
    return jnp.pad(w, [(0, 0)] * (w.ndim - 1) + [(0, cols - w.shape[-1])])


def kernel(x, norm_mix_g, w_in, mu_shift, w0, w_lora_up, a0, a_lora_up, g_lora_up, k_k, k_a, r_k,
           lnx_g, lnx_b, pool_w, pool_scale, w_up_a, w_up_b, w_o, norm_moe_g, w_router,
           w_gate_e, w_up_e, w_down_e, final_g):
    batch, seq, d = x.shape
    n = batch * seq
    depth = w_in.shape[0]
    d_pool = pool_w.shape[1] * pool_w.shape[2]
    lora = w_lora_up.shape[2]
    gate_lora = g_lora_up.shape[1]
    c_lora = 3 * D_RWKV
    c_pool = c_lora + 4 * lora + gate_lora
    c_ga = c_pool + d_pool
    c_gb = c_ga + d
    cap = CAPACITY_FACTOR * seq // N_EXPERTS

    def lora_cols(w):
        segs = [_pad_cols(w[..., c_lora + j * lora:c_lora + (j + 1) * lora], LORA_PAD) for j in range(4)]
        segs.append(_pad_cols(w[..., c_lora + 4 * lora:c_pool], GATE_PAD))
        return jnp.concatenate(segs, axis=-1)

    xf = x.reshape(n, d)
    for l in range(depth):
        wl = w_in[l]
        u = _rmsnorm(xf, norm_mix_g[l], BF16)
        p_rkv = _matmul(u, wl[:, :c_lora].astype(BF16), tn=1024, name="proj_rkv")
        p_lora = _matmul(u, lora_cols(wl).astype(BF16), tn=4 * LORA_PAD + GATE_PAD, name="proj_lora")
        p_pool = _matmul(u, wl[:, c_pool:c_ga].astype(BF16), tn=1024, name="proj_pool")
        mu = mu_shift[l]
        prep = _rwkv_prep(
            p_rkv, p_lora, mu[:c_lora].reshape(1, -1), lora_cols(mu).reshape(1, -1),
            w0[l], _pad_rows(w_lora_up[l], LORA_PAD).astype(BF16),
            a0[l], _pad_rows(a_lora_up[l], LORA_PAD).astype(BF16),
            _pad_rows(g_lora_up[l], GATE_PAD).astype(BF16),
            k_k[l].reshape(1, -1), k_a[l].reshape(1, -1), r_k[l].reshape(1, -1), seq)
        r, kk, v, lw_f, lw_b, g_f, g_b, k_f, k_b, b_f, b_b, g, bonus = prep
        yf, yb = _wkv(r, kk, v, lw_f, lw_b, g_f, g_b, k_f, k_b, b_f, b_b, batch)
        y_a = _rwkv_post(yf, yb, g, bonus, lnx_g[l].reshape(1, -1), lnx_b[l].reshape(1, -1))
        y_b = _pool(p_pool, pool_w[l].astype(BF16), pool_scale[l].reshape(1, -1), seq)
        merged = _merge(u, y_a, y_b, wl[:, c_ga:c_gb].astype(BF16), wl[:, c_gb:].astype(BF16),
                        w_up_a[l].astype(BF16), w_up_b[l].astype(BF16))
        xf = _matmul(merged, w_o[l].astype(BF16), residual=xf, name="out_proj")

        vn, aff = _router(xf, norm_moe_g[l], w_router[l])
        aff_t = jnp.swapaxes(aff.reshape(batch, seq, N_EXPERTS), 1, 2)
        gates, idx = lax.top_k(aff_t, cap)
        rows = idx + (jnp.arange(batch, dtype=idx.dtype) * seq)[:, None, None]
        rows = jnp.swapaxes(rows, 0, 1).reshape(N_EXPERTS, batch * cap).astype(jnp.int32)
        gates = jnp.swapaxes(gates, 0, 1).reshape(N_EXPERTS, batch * cap, 1)
        xf = _moe(xf, vn, rows, gates, w_gate_e, w_up_e, w_down_e, l)
    out = _rmsnorm(xf, final_g, F32)
    return out.reshape(batch, seq, d)
```
